```python
import jax, jax.numpy as jnp
from jax import lax
import numpy as np

D_MODEL = 1024
BATCH = 8
SEQ = 2048
DEPTH = 2
DEC_BATCH = 32
DEC_SEQ = 4
PAST_LEN = 8192
PAGE_SIZE = 128

N_HEADS = 16
HEAD_DIM = D_MODEL // N_HEADS
D_ATTN = N_HEADS * HEAD_DIM
D_CONV = D_MODEL
CONV_W = 31
D_FF = 2816
Q_BLOCK = 128
N_SUB = 3
EPS = 1e-6
FFN_RES = 0.5
NEG_INF = -1e30
FORGET_BIAS_LO = 2.0
FORGET_BIAS_HI = 10.0
IN_SIZES = (D_ATTN, D_ATTN, D_ATTN, N_HEADS, D_CONV, D_CONV, D_MODEL, D_MODEL)
IN_COLS = 3 * D_ATTN + N_HEADS + 2 * D_CONV + 2 * D_MODEL

kernel_name = "fox_conformer_macaron_decoder_step"


def rms_norm(x, g):
    xf = x.astype(jnp.float32)
    y = xf * lax.rsqrt(jnp.mean(xf * xf, axis=-1, keepdims=True) + EPS)
    return (y * g.astype(jnp.float32)).astype(x.dtype)


def layer_norm(x, g, b):
    xf = x.astype(jnp.float32)
    mu = jnp.mean(xf, axis=-1, keepdims=True)
    var = jnp.mean(jnp.square(xf - mu), axis=-1, keepdims=True)
    y = (xf - mu) * lax.rsqrt(var + EPS)
    return (y * g.astype(jnp.float32) + b.astype(jnp.float32)).astype(x.dtype)


def modulation(c, w_mod, b_mod):
    m = jax.nn.silu(c) @ w_mod + b_mod
    return m.reshape(c.shape[0], N_SUB, 3, D_MODEL)


def pre(x, g, m):
    return rms_norm(x, g) * (1.0 + m[:, 1][:, None]) + m[:, 0][:, None]


def post(x, y, g, m, res_w):
    return x + res_w * m[:, 2][:, None] * rms_norm(y, g)


def swiglu(h, wg, wu, wd):
    return (jax.nn.silu(h @ wg) * (h @ wu)) @ wd


def project_in(h, w_in, b_f):
    B, T, _ = h.shape
    proj = h @ w_in
    idx = [int(i) for i in np.cumsum(IN_SIZES)[:-1]]
    q, k, v, f, a, b, ga, gc = jnp.split(proj, idx, axis=-1)
    q = q.reshape(B, T, N_HEADS, HEAD_DIM)
    k = k.reshape(B, T, N_HEADS, HEAD_DIM)
    v = v.reshape(B, T, N_HEADS, HEAD_DIM)
    logf = jax.nn.log_sigmoid((f + b_f).astype(jnp.float32))
    u = a * jax.nn.sigmoid(b)
    return q, k, v, logf, u, ga, gc


def conv_branch(u, prev, conv_w, conv_b, ln_g, ln_b, w_pw):
    u_ext = jnp.concatenate([prev, u], axis=1)
    z = lax.conv_general_dilated(u_ext, conv_w[:, None, :], window_strides=(1,), padding='VALID',
                                 dimension_numbers=('NWC', 'WIO', 'NWC'),
                                 feature_group_count=D_CONV) + conv_b
    z = jax.nn.silu(layer_norm(z, ln_g, ln_b))
    return z @ w_pw, u_ext[:, -(CONV_W - 1):]


def fox_attention_prompt(q, k, v, logf):
    B, T = q.shape[0], q.shape[1]
    scale = HEAD_DIM ** -0.5
    F = jnp.cumsum(logf, axis=1)
    Ft = F.transpose(0, 2, 1)
    nb = T // Q_BLOCK
    qb = q.reshape(B, nb, Q_BLOCK, N_HEADS, HEAD_DIM).transpose(1, 0, 2, 3, 4)
    Fb = Ft.reshape(B, N_HEADS, nb, Q_BLOCK).transpose(2, 0, 1, 3)
    kpos = jnp.arange(T)

    def block(args):
        qi, Fi, i = args
        s = jnp.einsum('bqhd,bkhd->bhqk', qi, k).astype(jnp.float32) * scale
        s = s + Fi[..., None] - Ft[:, :, None, :]
        qpos = i * Q_BLOCK + jnp.arange(Q_BLOCK)
        s = jnp.where(kpos[None, :] <= qpos[:, None], s, NEG_INF)
        p = jax.nn.softmax(s, axis=-1).astype(v.dtype)
        return jnp.einsum('bhqk,bkhd->bqhd', p, v)

    o = lax.map(block, (qb, Fb, jnp.arange(nb)))
    return o.transpose(1, 0, 2, 3, 4).reshape(B, T, D_ATTN)


def fox_attention_sample(q, k, v, logf, k_past, v_past, logf_past):
    B, T = q.shape[0], q.shape[1]
    P = k_past.shape[1]
    scale = HEAD_DIM ** -0.5
    Fn = jnp.cumsum(logf, axis=1).transpose(0, 2, 1)
    lp = logf_past.astype(jnp.float32)
    decay_past = (lax.cumsum(lp, axis=1, reverse=True) - lp).transpose(0, 2, 1)
    s_past = jnp.einsum('bqhd,bkhd->bhqk', q, k_past).astype(jnp.float32) * scale
    s_past = s_past + Fn[..., None] + decay_past[:, :, None, :]
    s_new = jnp.einsum('bqhd,bkhd->bhqk', q, k).astype(jnp.float32) * scale
    s_new = s_new + Fn[..., None] - Fn[:, :, None, :]
    causal = jnp.arange(T)[None, :] <= jnp.arange(T)[:, None]
    s_new = jnp.where(causal, s_new, NEG_INF)
    p = jax.nn.softmax(jnp.concatenate([s_past, s_new], axis=-1), axis=-1).astype(v.dtype)
    o = (jnp.einsum('bhqk,bkhd->bqhd', p[..., :P], v_past)
         + jnp.einsum('bhqk,bkhd->bqhd', p[..., P:], v))
    return o.reshape(B, T, D_ATTN)


def merge_out(attn, conv_out, ga, gc, w_out):
    return (jax.nn.sigmoid(ga) * attn + jax.nn.sigmoid(gc) * conv_out) @ w_out


def mixer_prompt(h, w_in, b_f, conv_w, conv_b, ln_g, ln_b, w_pw, w_out):
    q, k, v, logf, u, ga, gc = project_in(h, w_in, b_f)
    attn = fox_attention_prompt(q, k, v, logf)
    prev = jnp.zeros((h.shape[0], CONV_W - 1, D_CONV), u.dtype)
    conv_out, conv_state = conv_branch(u, prev, conv_w, conv_b, ln_g, ln_b, w_pw)
    return merge_out(attn, conv_out, ga, gc, w_out), (k, v, logf.astype(h.dtype), conv_state)


def mixer_sample(h, k_past, v_past, logf_past, conv_prev, w_in, b_f, conv_w, conv_b, ln_g, ln_b, w_pw, w_out):
    q, k, v, logf, u, ga, gc = project_in(h, w_in, b_f)
    attn = fox_attention_sample(q, k, v, logf, k_past, v_past, logf_past)
    conv_out, conv_state = conv_branch(u, conv_prev, conv_w, conv_b, ln_g, ln_b, w_pw)
    return merge_out(attn, conv_out, ga, gc, w_out), (k, v, logf.astype(h.dtype), conv_state)


def layer(x, c, mixer, npre, npost, w_mod, b_mod, f1g, f1u, f1d, f2g, f2u, f2d):
    mod = modulation(c, w_mod, b_mod)
    h = pre(x, npre[0], mod[:, 0])
    x = post(x, swiglu(h, f1g, f1u, f1d), npost[0], mod[:, 0], FFN_RES)
    h = pre(x, npre[1], mod[:, 1])
    y, state = mixer(h)
    x = post(x, y, npost[1], mod[:, 1], 1.0)
    h = pre(x, npre[2], mod[:, 2])
    x = post(x, swiglu(h, f2g, f2u, f2d), npost[2], mod[:, 2], FFN_RES)
    return x, state


def setup_inputs(seed: int = 0) -> dict:
    key = jax.random.key(seed)
    ks = jax.random.split(key, 32)
    n_pages = PAST_LEN // PAGE_SIZE
    n_used = DEC_BATCH * n_pages
    n_phys = (n_used * 5) // 4

    def nrm(k, shape, s):
        return jax.random.normal(k, shape, jnp.float32) * s

    page_table = jax.random.permutation(ks[0], n_phys)[:n_used].reshape(DEC_BATCH, n_pages).astype(jnp.int32)
    head_bias = jnp.linspace(FORGET_BIAS_LO, FORGET_BIAS_HI, N_HEADS, dtype=jnp.float32)
    cache_logf = jax.nn.log_sigmoid(head_bias + nrm(ks[1], (DEPTH, n_phys, PAGE_SIZE, N_HEADS), 1.0))
    b_forget = head_bias[None, :] + nrm(ks[17], (DEPTH, N_HEADS), 0.3)
    return {
        "x_prompt": nrm(ks[2], (BATCH, SEQ, D_MODEL), 1.0),
        "x_sample": nrm(ks[3], (DEC_BATCH, DEC_SEQ, D_MODEL), 1.0),
        "c_prompt": nrm(ks[4], (BATCH, D_MODEL), 1.0),
        "c_sample": nrm(ks[5], (DEC_BATCH, D_MODEL), 1.0),
        "cache_k": nrm(ks[6], (DEPTH, n_phys, PAGE_SIZE, N_HEADS, HEAD_DIM), 1.0),
        "cache_v": nrm(ks[7], (DEPTH, n_phys, PAGE_SIZE, N_HEADS, HEAD_DIM), 1.0),
        "cache_logf": cache_logf,
        "state_conv": nrm(ks[8], (DEPTH, DEC_BATCH, CONV_W - 1, D_CONV), 0.5),
        "page_table": page_table,
        "norm_pre": 1.0 + nrm(ks[9], (DEPTH, N_SUB, D_MODEL), 0.1),
        "norm_post": 1.0 + nrm(ks[10], (DEPTH, N_SUB, D_MODEL), 0.1),
        "w_mod": nrm(ks[11], (DEPTH, D_MODEL, N_SUB * 3 * D_MODEL), 0.5 * D_MODEL ** -0.5),
        "b_mod": nrm(ks[12], (DEPTH, N_SUB * 3 * D_MODEL), 0.1),
        "ffn1_gate": nrm(ks[13], (DEPTH, D_MODEL, D_FF), D_MODEL ** -0.5),
        "ffn1_up": nrm(ks[14], (DEPTH, D_MODEL, D_FF), D_MODEL ** -0.5),
        "ffn1_down": nrm(ks[15], (DEPTH, D_FF, D_MODEL), D_FF ** -0.5),
        "w_in": nrm(ks[16], (DEPTH, D_MODEL, IN_COLS), D_MODEL ** -0.5),
        "b_forget": b_forget,
        "conv_w": nrm(ks[18], (DEPTH, CONV_W, D_CONV), CONV_W ** -0.5),
        "conv_b": nrm(ks[19], (DEPTH, D_CONV), 0.02),
        "conv_ln_g": 1.0 + nrm(ks[20], (DEPTH, D_CONV), 0.1),
        "conv_ln_b": nrm(ks[21], (DEPTH, D_CONV), 0.02),
        "w_pw": nrm(ks[22], (DEPTH, D_CONV, D_MODEL), D_CONV ** -0.5),
        "w_out": nrm(ks[23], (DEPTH, D_MODEL, D_MODEL), D_MODEL ** -0.5),
        "ffn2_gate": nrm(ks[24], (DEPTH, D_MODEL, D_FF), D_MODEL ** -0.5),
        "ffn2_up": nrm(ks[25], (DEPTH, D_MODEL, D_FF), D_MODEL ** -0.5),
        "ffn2_down": nrm(ks[26], (DEPTH, D_FF, D_MODEL), D_FF ** -0.5),
    }


def reference(x_prompt, x_sample, c_prompt, c_sample, cache_k, cache_v, cache_logf, state_conv, page_table,
              norm_pre, norm_post, w_mod, b_mod, ffn1_gate, ffn1_up, ffn1_down, w_in, b_forget,
              conv_w, conv_b, conv_ln_g, conv_ln_b, w_pw, w_out, ffn2_gate, ffn2_up, ffn2_down):
    db = x_sample.shape[0]
    xp, xs = x_prompt, x_sample
    kp, vp, lfp, cvp = [], [], [], []
    kss, vss, lfs, cvs = [], [], [], []
    for l in range(DEPTH):
        common = (norm_pre[l], norm_post[l], w_mod[l], b_mod[l],
                  ffn1_gate[l], ffn1_up[l], ffn1_down[l], ffn2_gate[l], ffn2_up[l], ffn2_down[l])
        mix = (w_in[l], b_forget[l], conv_w[l], conv_b[l], conv_ln_g[l], conv_ln_b[l], w_pw[l], w_out[l])
        k_past = cache_k[l][page_table].reshape(db, -1, N_HEADS, HEAD_DIM)
        v_past = cache_v[l][page_table].reshape(db, -1, N_HEADS, HEAD_DIM)
        lf_past = cache_logf[l][page_table].reshape(db, -1, N_HEADS)
        xp, (k1, v1, f1, c1) = layer(xp, c_prompt, lambda h: mixer_prompt(h, *mix), *common)
        xs, (k2, v2, f2, c2) = layer(
            xs, c_sample,
            lambda h: mixer_sample(h, k_past, v_past, lf_past, state_conv[l], *mix), *common)
        kp.append(k1); vp.append(v1); lfp.append(f1); cvp.append(c1)
        kss.append(k2); vss.append(v2); lfs.append(f2); cvs.append(c2)
    k_prompt = jnp.stack(kp, axis=0)
    v_prompt = jnp.stack(vp, axis=0)
    logf_prompt = jnp.stack(lfp, axis=0)
    conv_prompt = jnp.stack(cvp, axis=0)
    k_sample = jnp.stack(kss, axis=0)
    v_sample = jnp.stack(vss, axis=0)
    logf_sample = jnp.stack(lfs, axis=0)
    conv_sample = jnp.stack(cvs, axis=0)
    return (xp, xs, k_prompt, v_prompt, logf_prompt, conv_prompt, k_sample, v_sample, logf_sample, conv_sample)
```

```python
import functools

import jax
import jax.numpy as jnp
from jax import lax
from jax.experimental import pallas as pl
from jax.experimental.pallas import tpu as pltpu

F32 = jnp.float32
BF16 = jnp.bfloat16
HIGHEST = lax.Precision.HIGHEST

D_MODEL = 1024
N_HEADS = 16
HEAD_DIM = 64
D_FF = 2816
CONV_W = 31
N_SUB = 3
EPS = 1e-6
NEG_INF = -1e30
QK_SCALE = HEAD_DIM ** -0.5
LANES = 128
HEADS_PER_LANE_TILE = LANES // HEAD_DIM
N_LANE_TILES = D_MODEL // LANES
CONV_HALO = 32

VMEM_LIMIT = 56 * 1024 * 1024


def _cparams(sem):
    return pltpu.CompilerParams(dimension_semantics=sem, vmem_limit_bytes=VMEM_LIMIT)


def _const_spec(shape):
    nd = len(shape)
    return pl.BlockSpec(shape, lambda *_: (0,) * nd, pipeline_mode=pl.Buffered(1))


def _rms(x, g):
    return x * lax.rsqrt(jnp.mean(x * x, axis=-1, keepdims=True) + EPS) * g


def _dot(a, b, **kw):
    return jnp.dot(a, b, preferred_element_type=F32, **kw)


def _dot_nt(a, b, **kw):
    return lax.dot_general(a, b, (((1,), (1,)), ((), ())), preferred_element_type=F32, **kw)


def _mod_kernel(c_ref, w_ref, b_ref, o_ref):
    c = c_ref[...]
    a = (c * jax.nn.sigmoid(c)).astype(BF16)
    o_ref[0] = _dot(a, w_ref[0].astype(BF16)) + b_ref[0]


def _modulation(c_all, w_mod, b_mod):
    depth, _, n = w_mod.shape
    rows = c_all.shape[0]
    tn = 1152
    return pl.pallas_call(
        _mod_kernel,
        grid=(depth, n // tn),
        in_specs=[
            pl.BlockSpec((rows, D_MODEL), lambda l, j: (0, 0)),
            pl.BlockSpec((1, D_MODEL, tn), lambda l, j: (l, 0, j)),
            pl.BlockSpec((1, 1, tn), lambda l, j: (l, 0, j)),
        ],
        out_specs=pl.BlockSpec((1, rows, tn), lambda l, j: (l, 0, j)),
        out_shape=jax.ShapeDtypeStruct((depth, rows, n), F32),
        compiler_params=_cparams(("arbitrary", "arbitrary")),
        name="modulation",
    )(c_all, w_mod, b_mod.reshape(depth, 1, n))


def _mod_spec(m, tiles_per_block):
    rows = m.shape[1]
    return pl.BlockSpec((1, rows, D_MODEL), lambda i: (i // tiles_per_block, 0, 0))


def _ffn_kernel(x_ref, sh_ref, sc_ref, gt_ref, gpre_ref, gpost_ref, wg_ref, wu_ref, wd_ref, o_ref, *, res_w, ck):
    x = x_ref[...]
    h = (_rms(x, gpre_ref[...]) * (1.0 + sc_ref[0]) + sh_ref[0]).astype(BF16)
    acc = jnp.zeros(x.shape, F32)
    for c in range(D_FF // ck):
        g = _dot(h, wg_ref[:, c * ck:(c + 1) * ck])
        u = _dot(h, wu_ref[:, c * ck:(c + 1) * ck])
        a = (g * jax.nn.sigmoid(g) * u).astype(BF16)
        acc = acc + _dot(a, wd_ref[c * ck:(c + 1) * ck, :])
    o_ref[...] = x + res_w * gt_ref[0] * _rms(acc, gpost_ref[...])


def _ffn(x, mods, gpre, gpost, wg, wu, wd, *, tm, tiles_per_block, res_w):
    t = x.shape[0]
    sh, sc, gt = mods
    row = pl.BlockSpec((tm, D_MODEL), lambda i: (i, 0))
    return pl.pallas_call(
        functools.partial(_ffn_kernel, res_w=res_w, ck=D_FF // 2),
        grid=(t // tm,),
        in_specs=[row, _mod_spec(sh, tiles_per_block), _mod_spec(sc, tiles_per_block),
                  _mod_spec(gt, tiles_per_block), _const_spec((1, D_MODEL)), _const_spec((1, D_MODEL)),
                  _const_spec(wg.shape), _const_spec(wu.shape), _const_spec(wd.shape)],
        out_specs=row,
        out_shape=jax.ShapeDtypeStruct((t, D_MODEL), F32),
        compiler_params=_cparams(("arbitrary",)),
        name="ffn",
    )(x, sh, sc, gt, gpre, gpost, wg, wu, wd)


def _proj_kernel(x_ref, sh_ref, sc_ref, gpre_ref, wqkv_ref, wf_ref, wr_ref, bf_ref,
                 q_ref, k_ref, v_ref, kb_ref, vb_ref, lf_ref, fc_ref, u_ref, sga_ref, sgc_ref,
                 carry_ref, *, tiles_per_seq):
    x = x_ref[...]
    tm = x.shape[0]
    h = (_rms(x, gpre_ref[...]) * (1.0 + sc_ref[0]) + sh_ref[0]).astype(BF16)
    q_ref[...] = (_dot(h, wqkv_ref[:, 0:D_MODEL]) * QK_SCALE).astype(BF16)
    k = _dot(h, wqkv_ref[:, D_MODEL:2 * D_MODEL])
    k_ref[...] = k
    kb_ref[...] = k.astype(BF16)
    v = _dot(h, wqkv_ref[:, 2 * D_MODEL:3 * D_MODEL])
    v_ref[...] = v
    vb_ref[...] = v.astype(BF16)

    f = _dot(h, wf_ref[...]) + bf_ref[...]
    lf = jnp.minimum(f, 0.0) - jnp.log1p(jnp.exp(-jnp.abs(f)))
    lf_ref[...] = lf

    @pl.when(pl.program_id(0) % tiles_per_seq == 0)
    def _():
        carry_ref[...] = jnp.zeros(carry_ref.shape, F32)

    r = lax.broadcasted_iota(jnp.int32, (tm, tm), 0)
    c = lax.broadcasted_iota(jnp.int32, (tm, tm), 1)
    tri = (c <= r).astype(F32)
    fc = _dot(tri, lf, precision=HIGHEST) + carry_ref[...]
    fc_ref[...] = fc
    carry_ref[...] = fc[tm - 1:tm, :]

    a = _dot(h, wr_ref[:, 0:D_MODEL])
    b = _dot(h, wr_ref[:, D_MODEL:2 * D_MODEL])
    u_ref[...] = a * jax.nn.sigmoid(b)
    sga_ref[...] = jax.nn.sigmoid(_dot(h, wr_ref[:, 2 * D_MODEL:3 * D_MODEL])).astype(BF16)
    sgc_ref[...] = jax.nn.sigmoid(_dot(h, wr_ref[:, 3 * D_MODEL:4 * D_MODEL])).astype(BF16)


def _proj(x, mods, gpre, wqkv, wf, wr, bf, *, tm, tiles_per_block, tiles_per_seq):
    t = x.shape[0]
    sh, sc, _ = mods
    row = pl.BlockSpec((tm, D_MODEL), lambda i: (i, 0))
    hrow = pl.BlockSpec((tm, N_HEADS), lambda i: (i, 0))
    wide = lambda dt: jax.ShapeDtypeStruct((t, D_MODEL), dt)
    narrow = jax.ShapeDtypeStruct((t, N_HEADS), F32)
    return pl.pallas_call(
        functools.partial(_proj_kernel, tiles_per_seq=tiles_per_seq),
        grid=(t // tm,),
        in_specs=[row, _mod_spec(sh, tiles_per_block), _mod_spec(sc, tiles_per_block),
                  _const_spec((1, D_MODEL)), _const_spec(wqkv.shape), _const_spec(wf.shape),
                  _const_spec(wr.shape), _const_spec((1, N_HEADS))],
        out_specs=[row, row, row, row, row, hrow, hrow, row, row, row],
        out_shape=[wide(BF16), wide(F32), wide(F32), wide(BF16), wide(BF16), narrow, narrow,
                   wide(F32), wide(BF16), wide(BF16)],
        scratch_shapes=[pltpu.VMEM((1, N_HEADS), F32)],
        compiler_params=_cparams(("arbitrary",)),
        name="proj",
    )(x, sh, sc, gpre, wqkv, wf, wr, bf)


def _attn_prompt_kernel(q_ref, k_ref, v_ref, fcol_ref, frow_ref, o_ref, *, tq):
    qi = pl.program_id(1)
    lane = lax.broadcasted_iota(jnp.int32, (tq, LANES), 1)
    first = lane < HEAD_DIM
    rr = lax.broadcasted_iota(jnp.int32, (tq, tq), 0)
    cc = lax.broadcasted_iota(jnp.int32, (tq, tq), 1)
    causal = cc <= rr
    fcol = fcol_ref[0]

    for hp in range(N_LANE_TILES):
        cols = slice(hp * LANES, (hp + 1) * LANES)
        qp = q_ref[0, :, cols]
        zero = jnp.zeros_like(qp)
        q_heads = (jnp.where(first, qp, zero), jnp.where(first, zero, qp))
        f_i = tuple(fcol[:, HEADS_PER_LANE_TILE * hp + s:HEADS_PER_LANE_TILE * hp + s + 1]
                    for s in range(HEADS_PER_LANE_TILE))

        def step(j, carry, masked):
            kb = k_ref[0, pl.ds(pl.multiple_of(j * tq, tq), tq), cols]
            vb = v_ref[0, pl.ds(pl.multiple_of(j * tq, tq), tq), cols]
            out = []
            for s in range(HEADS_PER_LANE_TILE):
                m, l, acc = carry[s]
                f_j = frow_ref[0, HEADS_PER_LANE_TILE * hp + s, pl.ds(j, 1), :]
                sc = _dot_nt(q_heads[s], kb) + (f_i[s] - f_j)
                if masked:
                    sc = jnp.where(causal, sc, NEG_INF)
                m_new = jnp.maximum(m, jnp.max(sc, axis=-1, keepdims=True))
                alpha = jnp.exp(m - m_new)
                p = jnp.exp(sc - m_new)
                l = alpha * l + jnp.sum(p, axis=-1, keepdims=True)
                acc = alpha * acc + _dot(p.astype(BF16), vb)
                out.append((m_new, l, acc))
            return tuple(out)

        init = tuple((jnp.full((tq, 1), NEG_INF, F32), jnp.zeros((tq, 1), F32), jnp.zeros((tq, LANES), F32))
                     for _ in range(HEADS_PER_LANE_TILE))
        carry = lax.fori_loop(0, qi, lambda j, c: step(j, c, False), init)
        (_, l0, a0), (_, l1, a1) = step(qi, carry, True)
        o_ref[0, :, cols] = jnp.where(first, a0 / l0, a1 / l1).astype(o_ref.dtype)


def _attn_prompt(q, kb, vb, fcol, frow, *, tq):
    b, t, _ = q.shape
    nk = t // tq
    return pl.pallas_call(
        functools.partial(_attn_prompt_kernel, tq=tq),
        grid=(b, nk),
        in_specs=[
            pl.BlockSpec((1, tq, D_MODEL), lambda bi, i: (bi, i, 0)),
            pl.BlockSpec((1, t, D_MODEL), lambda bi, i: (bi, 0, 0)),
            pl.BlockSpec((1, t, D_MODEL), lambda bi, i: (bi, 0, 0)),
            pl.BlockSpec((1, tq, N_HEADS), lambda bi, i: (bi, i, 0)),
            pl.BlockSpec((1, N_HEADS, nk, tq), lambda bi, i: (bi, 0, 0, 0)),
        ],
        out_specs=pl.BlockSpec((1, tq, D_MODEL), lambda bi, i: (bi, i, 0)),
        out_shape=jax.ShapeDtypeStruct((b, t, D_MODEL), BF16),
        compiler_params=_cparams(("arbitrary", "arbitrary")),
        name="attn_prompt",
    )(q, kb, vb, fcol, frow)


def _layer_norm_swish(z, g, b):
    mu = jnp.mean(z, axis=-1, keepdims=True)
    zc = z - mu
    var = jnp.mean(zc * zc, axis=-1, keepdims=True)
    y = zc * lax.rsqrt(var + EPS) * g + b
    return y * jax.nn.sigmoid(y)


def _conv_prompt_kernel(u_ref, cw_ref, cb_ref, lng_ref, lnb_ref, z_ref, uext_ref, zs_ref, *, tm, rchunk):
    ti = pl.program_id(1)
    for cbk in range(N_LANE_TILES):
        @pl.when(ti == 0)
        def _():
            uext_ref[cbk, 0:CONV_HALO, :] = jnp.zeros((CONV_HALO, LANES), F32)

        @pl.when(ti > 0)
        def _():
            uext_ref[cbk, 0:CONV_HALO, :] = uext_ref[cbk, tm:tm + CONV_HALO, :]

    for cbk in range(N_LANE_TILES):
        uext_ref[cbk, CONV_HALO:CONV_HALO + tm, :] = u_ref[0, :, cbk * LANES:(cbk + 1) * LANES]

    first_tap = CONV_HALO - (CONV_W - 1)

    def body(cbk, _):
        w = cw_ref[cbk]
        for rb in range(tm // rchunk):
            acc = jnp.zeros((rchunk, LANES), F32)
            for j in range(CONV_W):
                start = rb * rchunk + first_tap + j
                acc = acc + w[j:j + 1, :] * uext_ref[cbk, pl.ds(start, rchunk), :]
            zs_ref[cbk, rb * rchunk:(rb + 1) * rchunk, :] = acc
        return 0

    lax.fori_loop(0, N_LANE_TILES, body, 0)
    z = jnp.concatenate([zs_ref[cbk] for cbk in range(N_LANE_TILES)], axis=-1) + cb_ref[...]
    z_ref[0] = _layer_norm_swish(z, lng_ref[...], lnb_ref[...]).astype(z_ref.dtype)


def _conv_prompt(u, cw3, cb, lng, lnb, *, tm):
    b, t, _ = u.shape
    row = pl.BlockSpec((1, tm, D_MODEL), lambda bi, i: (bi, i, 0))
    return pl.pallas_call(
        functools.partial(_conv_prompt_kernel, tm=tm, rchunk=64),
        grid=(b, t // tm),
        in_specs=[row, pl.BlockSpec(cw3.shape, lambda bi, i: (0, 0, 0)),
                  pl.BlockSpec((1, D_MODEL), lambda bi, i: (0, 0)),
                  pl.BlockSpec((1, D_MODEL), lambda bi, i: (0, 0)),
                  pl.BlockSpec((1, D_MODEL), lambda bi, i: (0, 0))],
        out_specs=row,
        out_shape=jax.ShapeDtypeStruct((b, t, D_MODEL), BF16),
        scratch_shapes=[pltpu.VMEM((N_LANE_TILES, CONV_HALO + tm, LANES), F32),
                        pltpu.VMEM((N_LANE_TILES, tm, LANES), F32)],
        compiler_params=_cparams(("arbitrary", "arbitrary")),
        name="conv_prompt",
    )(u, cw3, cb, lng, lnb)


def _conv_sample_kernel(uext_ref, cw_ref, cb_ref, lng_ref, lnb_ref, z_ref, *, n_new):
    acc = jnp.zeros((n_new, D_MODEL), F32)
    for j in range(CONV_W):
        acc = acc + cw_ref[j:j + 1, :] * uext_ref[0, j:j + n_new, :]
    z_ref[0] = _layer_norm_swish(acc + cb_ref[...], lng_ref[...], lnb_ref[...])


def _conv_sample(uext, cw, cb, lng, lnb, *, n_new):
    b, rows, _ = uext.shape
    vec = pl.BlockSpec((1, D_MODEL), lambda i: (0, 0))
    return pl.pallas_call(
        functools.partial(_conv_sample_kernel, n_new=n_new),
        grid=(b,),
        in_specs=[pl.BlockSpec((1, rows, D_MODEL), lambda i: (i, 0, 0)),
                  pl.BlockSpec((CONV_W, D_MODEL), lambda i: (0, 0)), vec, vec, vec],
        out_specs=pl.BlockSpec((1, n_new, D_MODEL), lambda i: (i, 0, 0)),
        out_shape=jax.ShapeDtypeStruct((b, n_new, D_MODEL), F32),
        compiler_params=_cparams(("arbitrary",)),
        name="conv_sample",
    )(uext, cw, cb, lng, lnb)


def _merge_kernel(x_ref, gt_ref, gpost_ref, zs_ref, at_ref, sga_ref, sgc_ref, wpw_ref, wout_ref, o_ref):
    conv_out = _dot(zs_ref[...].astype(BF16), wpw_ref[...])
    mix = sga_ref[...].astype(F32) * at_ref[...].astype(F32) + sgc_ref[...].astype(F32) * conv_out
    y = _dot(mix.astype(BF16), wout_ref[...])
    o_ref[...] = x_ref[...] + gt_ref[0] * _rms(y, gpost_ref[...])


def _merge(x, gt, gpost, zs, attn, sga, sgc, wpw, wout, *, tm, tiles_per_block):
    t = x.shape[0]
    row = pl.BlockSpec((tm, D_MODEL), lambda i: (i, 0))
    return pl.pallas_call(
        _merge_kernel,
        grid=(t // tm,),
        in_specs=[row, _mod_spec(gt, tiles_per_block), _const_spec((1, D_MODEL)), row, row, row, row,
                  _const_spec(wpw.shape), _const_spec(wout.shape)],
        out_specs=row,
        out_shape=jax.ShapeDtypeStruct((t, D_MODEL), F32),
        compiler_params=_cparams(("arbitrary",)),
        name="merge",
    )(x, gt, gpost, zs, attn, sga, sgc, wpw, wout)


def _attn_sample_kernel(pt_ref, q_ref, kn_ref, vn_ref, lfn_ref, *rest, n_new, ppg, page):
    del pt_ref
    k_refs = rest[0:ppg]
    v_refs = rest[ppg:2 * ppg]
    lf_refs = rest[2 * ppg:3 * ppg]
    o_ref = rest[3 * ppg]
    qm_ref, m_ref, l_ref, acc_ref, carry_ref, fn_ref = rest[3 * ppg + 1:]
    g = pl.program_id(1)
    rows = n_new * N_HEADS
    kpad = kn_ref.shape[1]

    er = lax.broadcasted_iota(jnp.int32, (rows, N_HEADS), 0)
    ec = lax.broadcasted_iota(jnp.int32, (rows, N_HEADS), 1)
    expand = ((er % N_HEADS) == ec).astype(F32)

    @pl.when(g == 0)
    def _():
        q4 = q_ref[0]
        qrep = jnp.concatenate([jnp.broadcast_to(q4[t:t + 1, :], (N_HEADS, D_MODEL)) for t in range(n_new)], axis=0)
        r = lax.broadcasted_iota(jnp.int32, (rows, D_MODEL), 0)
        c = lax.broadcasted_iota(jnp.int32, (rows, D_MODEL), 1)
        qm = jnp.where((c // HEAD_DIM) == (r % N_HEADS), qrep, 0.0).astype(BF16)
        qm_ref[...] = qm

        lfn = lfn_ref[0]
        run = lfn[0:1, :]
        cums = [run]
        for t in range(1, n_new):
            run = run + lfn[t:t + 1, :]
            cums.append(run)
        fn = jnp.concatenate(cums + [jnp.zeros((kpad - n_new, N_HEADS), F32)], axis=0)
        fnt = _dot_nt(expand, fn, precision=HIGHEST)
        trow = lax.broadcasted_iota(jnp.int32, (rows, kpad), 0) // N_HEADS
        tcol = lax.broadcasted_iota(jnp.int32, (rows, kpad), 1)
        fcol = jnp.sum(jnp.where(tcol == trow, fnt, 0.0), axis=-1, keepdims=True)
        fn_ref[...] = fcol
        s = _dot_nt(qm, kn_ref[0]) + (fcol - fnt)
        s = jnp.where(tcol <= trow, s, NEG_INF)
        m = jnp.max(s, axis=-1, keepdims=True)
        p = jnp.exp(s - m)
        m_ref[...] = m
        l_ref[...] = jnp.sum(p, axis=-1, keepdims=True)
        acc_ref[...] = _dot(p.astype(BF16), vn_ref[0])
        carry_ref[...] = jnp.zeros(carry_ref.shape, F32)

    jr = lax.broadcasted_iota(jnp.int32, (page, page), 0)
    ic = lax.broadcasted_iota(jnp.int32, (page, page), 1)
    later = (jr > ic).astype(F32)
    qm = qm_ref[...]
    fcol = fn_ref[...]
    for r in range(ppg):
        kp = k_refs[r][0, 0].astype(BF16)
        vp = v_refs[r][0, 0].astype(BF16)
        lf = lf_refs[r][0, 0]
        lft = _dot_nt(expand, lf, precision=HIGHEST)
        carry = carry_ref[...]
        decay = _dot(lft, later, precision=HIGHEST) + carry
        carry_ref[...] = carry + jnp.sum(lft, axis=-1, keepdims=True)
        s = _dot_nt(qm, kp) + (fcol + decay)
        m = m_ref[...]
        m_new = jnp.maximum(m, jnp.max(s, axis=-1, keepdims=True))
        alpha = jnp.exp(m - m_new)
        p = jnp.exp(s - m_new)
        m_ref[...] = m_new
        l_ref[...] = alpha * l_ref[...] + jnp.sum(p, axis=-1, keepdims=True)
        acc_ref[...] = alpha * acc_ref[...] + _dot(p.astype(BF16), vp)

    @pl.when(g == pl.num_programs(1) - 1)
    def _():
        o = acc_ref[...] / l_ref[...]
        r = lax.broadcasted_iota(jnp.int32, (rows, D_MODEL), 0)
        c = lax.broadcasted_iota(jnp.int32, (rows, D_MODEL), 1)
        o = jnp.where((c // HEAD_DIM) == (r % N_HEADS), o, 0.0)
        o_ref[0] = jnp.concatenate(
            [jnp.sum(o[t * N_HEADS:(t + 1) * N_HEADS, :], axis=0, keepdims=True) for t in range(n_new)], axis=0)


def _attn_sample(layer, page_table, q, kn, vn, lfn, cache_k, cache_v, cache_lf, *, ppg):
    b, n_new, _ = q.shape
    n_pages = page_table.shape[1]
    page = cache_k.shape[2]
    kpad = kn.shape[1]
    rows = n_new * N_HEADS
    groups = n_pages // ppg

    def page_spec(r, width):
        def imap(s, g, pt):
            return (layer, pt[s * n_pages + (n_pages - 1 - (g * ppg + r))], 0, 0)
        return pl.BlockSpec((1, 1, page, width), imap)

    per_seq = lambda nrow, width: pl.BlockSpec((1, nrow, width), lambda s, g, pt: (s, 0, 0))
    in_specs = ([per_seq(n_new, D_MODEL), per_seq(kpad, D_MODEL), per_seq(kpad, D_MODEL), per_seq(n_new, N_HEADS)]
                + [page_spec(r, D_MODEL) for r in range(ppg)]
                + [page_spec(r, D_MODEL) for r in range(ppg)]
                + [page_spec(r, N_HEADS) for r in range(ppg)])
    grid_spec = pltpu.PrefetchScalarGridSpec(
        num_scalar_prefetch=1,
        grid=(b, groups),
        in_specs=in_specs,
        out_specs=per_seq(n_new, D_MODEL),
        scratch_shapes=[pltpu.VMEM((rows, D_MODEL), BF16), pltpu.VMEM((rows, 1), F32), pltpu.VMEM((rows, 1), F32),
                        pltpu.VMEM((rows, D_MODEL), F32), pltpu.VMEM((rows, 1), F32), pltpu.VMEM((rows, 1), F32)],
    )
    return pl.pallas_call(
        functools.partial(_attn_sample_kernel, n_new=n_new, ppg=ppg, page=page),
        grid_spec=grid_spec,
        out_shape=jax.ShapeDtypeStruct((b, n_new, D_MODEL), F32),
        compiler_params=_cparams(("arbitrary", "arbitrary")),
        name="attn_sample",
    )(page_table.reshape(-1), q, kn, vn, lfn, *([cache_k] * ppg), *([cache_v] * ppg), *([cache_lf] * ppg))


def kernel(x_prompt, x_sample, c_prompt, c_sample, cache_k, cache_v, cache_logf, state_conv, page_table,
           norm_pre, norm_post, w_mod, b_mod, ffn1_gate, ffn1_up, ffn1_down, w_in, b_forget,
           conv_w, conv_b, conv_ln_g, conv_ln_b, w_pw, w_out, ffn2_gate, ffn2_up, ffn2_down):
    depth = w_mod.shape[0]
    bp, seq, _ = x_prompt.shape
    bs, n_new, _ = x_sample.shape
    n_phys, page = cache_k.shape[1], cache_k.shape[2]
    tp, ts = bp * seq, bs * n_new
    d_attn = N_HEADS * HEAD_DIM
    assert d_attn == D_MODEL and w_in.shape[2] == 3 * d_attn + N_HEADS + 4 * D_MODEL

    tm_ffn, tm_proj, tq, tm_conv, tm_merge = 512, 256, 256, 256, 512
    kpad = 16

    mod = _modulation(jnp.concatenate([c_prompt, c_sample], axis=0), w_mod, b_mod)
    mod = mod.reshape(depth, bp + bs, N_SUB, 3, D_MODEL)

    ck = cache_k.reshape(depth, n_phys, page, d_attn)
    cv = cache_v.reshape(depth, n_phys, page, d_attn)

    xp = x_prompt.reshape(tp, D_MODEL)
    xs = x_sample.reshape(ts, D_MODEL)
    outs = [[] for _ in range(8)]
    for l in range(depth):
        wqkv = w_in[l, :, :3 * d_attn].astype(BF16)
        wf = w_in[l, :, 3 * d_attn:3 * d_attn + N_HEADS].astype(BF16)
        wr = w_in[l, :, 3 * d_attn + N_HEADS:].astype(BF16)
        bf = b_forget[l].reshape(1, N_HEADS)
        f1 = (ffn1_gate[l].astype(BF16), ffn1_up[l].astype(BF16), ffn1_down[l].astype(BF16))
        f2 = (ffn2_gate[l].astype(BF16), ffn2_up[l].astype(BF16), ffn2_down[l].astype(BF16))
        wpw = w_pw[l].astype(BF16)
        wout = w_out[l].astype(BF16)
        gpre = [norm_pre[l, s].reshape(1, D_MODEL) for s in range(N_SUB)]
        gpost = [norm_post[l, s].reshape(1, D_MODEL) for s in range(N_SUB)]
        cb = conv_b[l].reshape(1, D_MODEL)
        lng = conv_ln_g[l].reshape(1, D_MODEL)
        lnb = conv_ln_b[l].reshape(1, D_MODEL)
        cw3 = conv_w[l].reshape(CONV_W, N_LANE_TILES, LANES).transpose(1, 0, 2)

        mods_p = [[mod[l, :bp, s, j].reshape(bp, 1, D_MODEL) for j in range(3)] for s in range(N_SUB)]
        mods_s = [[jnp.repeat(mod[l, bp:, s, j], n_new, axis=0).reshape(1, ts, D_MODEL) for j in range(3)]
                  for s in range(N_SUB)]

        xp = _ffn(xp, mods_p[0], gpre[0], gpost[0], *f1, tm=tm_ffn, tiles_per_block=seq // tm_ffn, res_w=0.5)
        q, k, v, kb, vb, lf, fc, u, sga, sgc = _proj(
            xp, mods_p[1], gpre[1], wqkv, wf, wr, bf, tm=tm_proj, tiles_per_block=seq // tm_proj,
            tiles_per_seq=seq // tm_proj)
        frow = fc.reshape(bp, seq, N_HEADS).transpose(0, 2, 1).reshape(bp, N_HEADS, seq // tq, tq)
        attn = _attn_prompt(q.reshape(bp, seq, D_MODEL), kb.reshape(bp, seq, D_MODEL), vb.reshape(bp, seq, D_MODEL),
                            fc.reshape(bp, seq, N_HEADS), frow, tq=tq)
        u3 = u.reshape(bp, seq, D_MODEL)
        zs = _conv_prompt(u3, cw3, cb, lng, lnb, tm=tm_conv)
        xp = _merge(xp, mods_p[1][2], gpost[1], zs.reshape(tp, D_MODEL), attn.reshape(tp, D_MODEL), sga, sgc,
                    wpw, wout, tm=tm_merge, tiles_per_block=seq // tm_merge)
        xp = _ffn(xp, mods_p[2], gpre[2], gpost[2], *f2, tm=tm_ffn, tiles_per_block=seq // tm_ffn, res_w=0.5)
        outs[0].append(k.reshape(bp, seq, N_HEADS, HEAD_DIM))
        outs[1].append(v.reshape(bp, seq, N_HEADS, HEAD_DIM))
        outs[2].append(lf.reshape(bp, seq, N_HEADS))
        outs[3].append(u3[:, seq - (CONV_W - 1):, :])

        xs = _ffn(xs, mods_s[0], gpre[0], gpost[0], *f1, tm=ts, tiles_per_block=1, res_w=0.5)
        q, k, v, kb, vb, lf, _, u, sga, sgc = _proj(
            xs, mods_s[1], gpre[1], wqkv, wf, wr, bf, tm=ts, tiles_per_block=1, tiles_per_seq=1)
        pad = lambda a: jnp.pad(a.reshape(bs, n_new, D_MODEL), ((0, 0), (0, kpad - n_new), (0, 0)))
        attn = _attn_sample(l, page_table, (q.astype(F32)).reshape(bs, n_new, D_MODEL), pad(kb), pad(vb),
                            lf.reshape(bs, n_new, N_HEADS), ck, cv, cache_logf, ppg=8)
        uext = jnp.concatenate([state_conv[l], u.reshape(bs, n_new, D_MODEL)], axis=1)
        zs = _conv_sample(uext, conv_w[l], cb, lng, lnb, n_new=n_new)
        xs = _merge(xs, mods_s[1][2], gpost[1], zs.reshape(ts, D_MODEL), attn.reshape(ts, D_MODEL), sga, sgc,
                    wpw, wout, tm=ts, tiles_per_block=1)
        xs = _ffn(xs, mods_s[2], gpre[2], gpost[2], *f2, tm=ts, tiles_per_block=1, res_w=0.5)
        outs[4].append(k.reshape(bs, n_new, N_HEADS, HEAD_DIM))
        outs[5].append(v.reshape(bs, n_new, N_HEADS, HEAD_DIM))
        outs[6].append(lf.reshape(bs, n_new, N_HEADS))
        outs[7].append(uext[:, n_new:, :])

    stacked = [jnp.stack(o, axis=0) for o in outs]
    return (xp.reshape(bp, seq, D_MODEL), xs.reshape(bs, n_new, D_MODEL), *stacked)
```

```python
import functools

import jax
import jax.numpy as jnp
from jax import lax
from jax.experimental import pallas as pl
from jax.experimental.pallas import tpu as pltpu

F32 = jnp.float32
BF16 = jnp.bfloat16
HIGHEST = lax.Precision.HIGHEST

D_MODEL = 1024
N_HEADS = 16
HEAD_DIM = 64
D_FF = 2816
CONV_W = 31
N_SUB = 3
EPS = 1e-6
NEG_INF = -1e30
QK_SCALE = HEAD_DIM ** -0.5
LOG2E = 1.4426950408889634
LANES = 128
HEADS_PER_LANE_TILE = LANES // HEAD_DIM
N_LANE_TILES = D_MODEL // LANES
CONV_HALO = 32

VMEM_LIMIT = 56 * 1024 * 1024


def _cparams(sem):
    return pltpu.CompilerParams(dimension_semantics=sem, vmem_limit_bytes=VMEM_LIMIT)


def _const_spec(shape):
    nd = len(shape)
    return pl.BlockSpec(shape, lambda *_: (0,) * nd, pipeline_mode=pl.Buffered(1))


def _rms(x, g):
    return x * lax.rsqrt(jnp.mean(x * x, axis=-1, keepdims=True) + EPS) * g


def _dot(a, b, **kw):
    return jnp.dot(a, b, preferred_element_type=F32, **kw)


def _dot_nt(a, b, **kw):
    return lax.dot_general(a, b, (((1,), (1,)), ((), ())), preferred_element_type=F32, **kw)


def _split3(x):
    hi = x.astype(BF16)
    r = x - hi.astype(F32)
    mid = r.astype(BF16)
    lo = (r - mid.astype(F32)).astype(BF16)
    return hi, mid, lo


def _mod_kernel(c_ref, w_ref, b_ref, o_ref):
    c = c_ref[...]
    a = (c * jax.nn.sigmoid(c)).astype(BF16)
    o_ref[0] = _dot(a, w_ref[0].astype(BF16)) + b_ref[0]


def _modulation(c_all, w_mod, b_mod):
    depth, _, n = w_mod.shape
    rows = c_all.shape[0]
    tn = 1152
    return pl.pallas_call(
        _mod_kernel,
        grid=(depth, n // tn),
        in_specs=[
            pl.BlockSpec((rows, D_MODEL), lambda l, j: (0, 0)),
            pl.BlockSpec((1, D_MODEL, tn), lambda l, j: (l, 0, j)),
            pl.BlockSpec((1, 1, tn), lambda l, j: (l, 0, j)),
        ],
        out_specs=pl.BlockSpec((1, rows, tn), lambda l, j: (l, 0, j)),
        out_shape=jax.ShapeDtypeStruct((depth, rows, n), F32),
        compiler_params=_cparams(("arbitrary", "arbitrary")),
        name="modulation",
    )(c_all, w_mod, b_mod.reshape(depth, 1, n))


def _mod_spec(m, tiles_per_block):
    rows = m.shape[1]
    return pl.BlockSpec((1, rows, D_MODEL), lambda i: (i // tiles_per_block, 0, 0))


def _ffn_kernel(x_ref, sh_ref, sc_ref, gt_ref, gpre_ref, gpost_ref, wg_ref, wu_ref, wd_ref, o_ref, *, res_w, ck):
    x = x_ref[...]
    h = (_rms(x, gpre_ref[...]) * (1.0 + sc_ref[0]) + sh_ref[0]).astype(BF16)
    acc = jnp.zeros(x.shape, F32)
    for c in range(D_FF // ck):
        g = _dot(h, wg_ref[:, c * ck:(c + 1) * ck])
        u = _dot(h, wu_ref[:, c * ck:(c + 1) * ck])
        a = (g * jax.nn.sigmoid(g) * u).astype(BF16)
        acc = acc + _dot(a, wd_ref[c * ck:(c + 1) * ck, :])
    o_ref[...] = x + res_w * gt_ref[0] * _rms(acc, gpost_ref[...])


def _ffn(x, mods, gpre, gpost, wg, wu, wd, *, tm, tiles_per_block, res_w):
    t = x.shape[0]
    sh, sc, gt = mods
    row = pl.BlockSpec((tm, D_MODEL), lambda i: (i, 0))
    return pl.pallas_call(
        functools.partial(_ffn_kernel, res_w=res_w, ck=D_FF // 2),
        grid=(t // tm,),
        in_specs=[row, _mod_spec(sh, tiles_per_block), _mod_spec(sc, tiles_per_block),
                  _mod_spec(gt, tiles_per_block), _const_spec((1, D_MODEL)), _const_spec((1, D_MODEL)),
                  _const_spec(wg.shape), _const_spec(wu.shape), _const_spec(wd.shape)],
        out_specs=row,
        out_shape=jax.ShapeDtypeStruct((t, D_MODEL), F32),
        compiler_params=_cparams(("arbitrary",)),
        name="ffn",
    )(x, sh, sc, gt, gpre, gpost, wg, wu, wd)


def _proj_kernel(x_ref, sh_ref, sc_ref, gpre_ref, wqkv_ref, wf_ref, wr_ref, bf_ref,
                 q_ref, k_ref, v_ref, kb_ref, vb_ref, vt_ref, lf_ref, fc_ref, u_ref, sga_ref, sgc_ref,
                 carry_ref, *, tiles_per_seq, q_scale):
    x = x_ref[...]
    tm = x.shape[0]
    h = (_rms(x, gpre_ref[...]) * (1.0 + sc_ref[0]) + sh_ref[0]).astype(BF16)
    q_ref[...] = (_dot(h, wqkv_ref[:, 0:D_MODEL]) * q_scale).astype(BF16)
    k = _dot(h, wqkv_ref[:, D_MODEL:2 * D_MODEL])
    k_ref[...] = k
    kb_ref[...] = k.astype(BF16)
    v = _dot(h, wqkv_ref[:, 2 * D_MODEL:3 * D_MODEL])
    v_ref[...] = v
    vb_ref[...] = v.astype(BF16)
    vt_ref[0] = v.T.astype(BF16)

    f = _dot(h, wf_ref[...]) + bf_ref[...]
    lf = jnp.minimum(f, 0.0) - jnp.log1p(jnp.exp(-jnp.abs(f)))
    lf_ref[...] = lf

    @pl.when(pl.program_id(0) % tiles_per_seq == 0)
    def _():
        carry_ref[...] = jnp.zeros(carry_ref.shape, F32)

    r = lax.broadcasted_iota(jnp.int32, (tm, tm), 0)
    c = lax.broadcasted_iota(jnp.int32, (tm, tm), 1)
    tri = (c <= r).astype(F32)
    fc = _dot(tri, lf, precision=HIGHEST) + carry_ref[...]
    fc_ref[...] = fc
    carry_ref[...] = fc[tm - 1:tm, :]

    a = _dot(h, wr_ref[:, 0:D_MODEL])
    b = _dot(h, wr_ref[:, D_MODEL:2 * D_MODEL])
    u_ref[...] = a * jax.nn.sigmoid(b)
    sga_ref[...] = jax.nn.sigmoid(_dot(h, wr_ref[:, 2 * D_MODEL:3 * D_MODEL])).astype(BF16)
    sgc_ref[...] = jax.nn.sigmoid(_dot(h, wr_ref[:, 3 * D_MODEL:4 * D_MODEL])).astype(BF16)


def _proj(x, mods, gpre, wqkv, wf, wr, bf, *, tm, tiles_per_block, tiles_per_seq, q_scale):
    t = x.shape[0]
    sh, sc, _ = mods
    row = pl.BlockSpec((tm, D_MODEL), lambda i: (i, 0))
    hrow = pl.BlockSpec((tm, N_HEADS), lambda i: (i, 0))
    trow = pl.BlockSpec((1, D_MODEL, tm), lambda i: (i, 0, 0))
    wide = lambda dt: jax.ShapeDtypeStruct((t, D_MODEL), dt)
    narrow = jax.ShapeDtypeStruct((t, N_HEADS), F32)
    return pl.pallas_call(
        functools.partial(_proj_kernel, tiles_per_seq=tiles_per_seq, q_scale=q_scale),
        grid=(t // tm,),
        in_specs=[row, _mod_spec(sh, tiles_per_block), _mod_spec(sc, tiles_per_block),
                  _const_spec((1, D_MODEL)), _const_spec(wqkv.shape), _const_spec(wf.shape),
                  _const_spec(wr.shape), _const_spec((1, N_HEADS))],
        out_specs=[row, row, row, row, row, trow, hrow, hrow, row, row, row],
        out_shape=[wide(BF16), wide(F32), wide(F32), wide(BF16), wide(BF16),
                   jax.ShapeDtypeStruct((t // tm, D_MODEL, tm), BF16), narrow, narrow,
                   wide(F32), wide(BF16), wide(BF16)],
        scratch_shapes=[pltpu.VMEM((1, N_HEADS), F32)],
        compiler_params=_cparams(("arbitrary",)),
        name="proj",
    )(x, sh, sc, gpre, wqkv, wf, wr, bf)


def _attn_prompt_kernel(q_ref, k_ref, vt_ref, frow_ref, fcol_ref, o_ref, qm_ref, m_ref, l_ref, acc_ref, *s_refs, tq):
    qi = pl.program_id(1)
    lane = lax.broadcasted_iota(jnp.int32, (tq, LANES), 1)
    first = lane < HEAD_DIM
    key_idx = lax.broadcasted_iota(jnp.int32, (tq, tq), 0)
    qry_idx = lax.broadcasted_iota(jnp.int32, (tq, tq), 1)
    causal = key_idx <= qry_idx

    for hp in range(N_LANE_TILES):
        qp = q_ref[0, :, hp * LANES:(hp + 1) * LANES]
        zero = jnp.zeros_like(qp)
        qm_ref[HEADS_PER_LANE_TILE * hp] = jnp.where(first, qp, zero)
        qm_ref[HEADS_PER_LANE_TILE * hp + 1] = jnp.where(first, zero, qp)
    m_ref[...] = jnp.full(m_ref.shape, NEG_INF, F32)
    l_ref[...] = jnp.zeros(l_ref.shape, F32)
    acc_ref[...] = jnp.zeros(acc_ref.shape, F32)

    def scores(j):
        start = pl.multiple_of(j * tq, tq)
        for h in range(N_HEADS):
            hp = h // HEADS_PER_LANE_TILE
            kb = k_ref[0, pl.ds(start, tq), hp * LANES:(hp + 1) * LANES]
            s_refs[h][...] = _dot_nt(kb, qm_ref[h])

    def softmax_pv(j, masked):
        start = pl.multiple_of(j * tq, tq)
        for h in range(N_HEADS):
            f_i = frow_ref[0, h:h + 1, :] * LOG2E
            f_j = fcol_ref[0, pl.ds(start, tq), h:h + 1] * LOG2E
            st = s_refs[h][...] + (f_i - f_j)
            if masked:
                st = jnp.where(causal, st, NEG_INF)
            m = m_ref[h]
            m_new = jnp.maximum(m, jnp.max(st, axis=0, keepdims=True))
            alpha = jnp.exp2(m - m_new)
            p = jnp.exp2(st - m_new)
            m_ref[h] = m_new
            l_ref[h] = alpha * l_ref[h] + jnp.sum(p, axis=0, keepdims=True)
            vt = vt_ref[0, j, h * HEAD_DIM:(h + 1) * HEAD_DIM, :]
            acc_ref[h] = alpha * acc_ref[h] + _dot(vt, p.astype(BF16))

    def body(j, c):
        softmax_pv(j - 1, False)
        scores(j)
        return c

    scores(0)
    lax.fori_loop(1, qi + 1, body, 0)
    softmax_pv(qi, True)
    for hp in range(N_LANE_TILES):
        h0 = HEADS_PER_LANE_TILE * hp
        ot = jnp.concatenate([acc_ref[h0] / l_ref[h0], acc_ref[h0 + 1] / l_ref[h0 + 1]], axis=0)
        o_ref[0, :, hp * LANES:(hp + 1) * LANES] = ot.T.astype(o_ref.dtype)


def _attn_prompt(q, kb, vt, frow, fcol, *, tq):
    b, t, _ = q.shape
    nk = t // tq
    return pl.pallas_call(
        functools.partial(_attn_prompt_kernel, tq=tq),
        grid=(b, nk),
        in_specs=[
            pl.BlockSpec((1, tq, D_MODEL), lambda bi, i: (bi, i, 0)),
            pl.BlockSpec((1, t, D_MODEL), lambda bi, i: (bi, 0, 0)),
            pl.BlockSpec((1, nk, D_MODEL, tq), lambda bi, i: (bi, 0, 0, 0)),
            pl.BlockSpec((1, N_HEADS, tq), lambda bi, i: (bi, 0, i)),
            pl.BlockSpec((1, t, N_HEADS), lambda bi, i: (bi, 0, 0)),
        ],
        out_specs=pl.BlockSpec((1, tq, D_MODEL), lambda bi, i: (bi, i, 0)),
        out_shape=jax.ShapeDtypeStruct((b, t, D_MODEL), BF16),
        scratch_shapes=[pltpu.VMEM((N_HEADS, tq, LANES), BF16), pltpu.VMEM((N_HEADS, 1, tq), F32),
                        pltpu.VMEM((N_HEADS, 1, tq), F32), pltpu.VMEM((N_HEADS, HEAD_DIM, tq), F32)]
        + [pltpu.VMEM((tq, tq), F32) for _ in range(N_HEADS)],
        compiler_params=_cparams(("arbitrary", "arbitrary")),
        name="attn_prompt",
    )(q, kb, vt, frow, fcol)


def _layer_norm_swish(z, g, b):
    mu = jnp.mean(z, axis=-1, keepdims=True)
    zc = z - mu
    var = jnp.mean(zc * zc, axis=-1, keepdims=True)
    y = zc * lax.rsqrt(var + EPS) * g + b
    return y * jax.nn.sigmoid(y)


def _conv_prompt_kernel(u_ref, cw_ref, cb_ref, lng_ref, lnb_ref, z_ref, uext_ref, zs_ref, *, tm, rchunk):
    ti = pl.program_id(1)
    for cbk in range(N_LANE_TILES):
        @pl.when(ti == 0)
        def _():
            uext_ref[cbk, 0:CONV_HALO, :] = jnp.zeros((CONV_HALO, LANES), F32)

        @pl.when(ti > 0)
        def _():
            uext_ref[cbk, 0:CONV_HALO, :] = uext_ref[cbk, tm:tm + CONV_HALO, :]

    for cbk in range(N_LANE_TILES):
        uext_ref[cbk, CONV_HALO:CONV_HALO + tm, :] = u_ref[0, :, cbk * LANES:(cbk + 1) * LANES]

    first_tap = CONV_HALO - (CONV_W - 1)

    def body(cbk, _):
        w = cw_ref[cbk]
        for rb in range(tm // rchunk):
            acc = jnp.zeros((rchunk, LANES), F32)
            for j in range(CONV_W):
                start = rb * rchunk + first_tap + j
                acc = acc + w[j:j + 1, :] * uext_ref[cbk, pl.ds(start, rchunk), :]
            zs_ref[cbk, rb * rchunk:(rb + 1) * rchunk, :] = acc
        return 0

    lax.fori_loop(0, N_LANE_TILES, body, 0)
    z = jnp.concatenate([zs_ref[cbk] for cbk in range(N_LANE_TILES)], axis=-1) + cb_ref[...]
    z_ref[0] = _layer_norm_swish(z, lng_ref[...], lnb_ref[...]).astype(z_ref.dtype)


def _conv_prompt(u, cw3, cb, lng, lnb, *, tm):
    b, t, _ = u.shape
    row = pl.BlockSpec((1, tm, D_MODEL), lambda bi, i: (bi, i, 0))
    return pl.pallas_call(
        functools.partial(_conv_prompt_kernel, tm=tm, rchunk=64),
        grid=(b, t // tm),
        in_specs=[row, pl.BlockSpec(cw3.shape, lambda bi, i: (0, 0, 0)),
                  pl.BlockSpec((1, D_MODEL), lambda bi, i: (0, 0)),
                  pl.BlockSpec((1, D_MODEL), lambda bi, i: (0, 0)),
                  pl.BlockSpec((1, D_MODEL), lambda bi, i: (0, 0))],
        out_specs=row,
        out_shape=jax.ShapeDtypeStruct((b, t, D_MODEL), BF16),
        scratch_shapes=[pltpu.VMEM((N_LANE_TILES, CONV_HALO + tm, LANES), F32),
                        pltpu.VMEM((N_LANE_TILES, tm, LANES), F32)],
        compiler_params=_cparams(("arbitrary", "arbitrary")),
        name="conv_prompt",
    )(u, cw3, cb, lng, lnb)


def _conv_sample_kernel(uext_ref, cw_ref, cb_ref, lng_ref, lnb_ref, z_ref, *, n_new):
    acc = jnp.zeros((n_new, D_MODEL), F32)
    for j in range(CONV_W):
        acc = acc + cw_ref[j:j + 1, :] * uext_ref[0, j:j + n_new, :]
    z_ref[0] = _layer_norm_swish(acc + cb_ref[...], lng_ref[...], lnb_ref[...])


def _conv_sample(uext, cw, cb, lng, lnb, *, n_new):
    b, rows, _ = uext.shape
    vec = pl.BlockSpec((1, D_MODEL), lambda i: (0, 0))
    return pl.pallas_call(
        functools.partial(_conv_sample_kernel, n_new=n_new),
        grid=(b,),
        in_specs=[pl.BlockSpec((1, rows, D_MODEL), lambda i: (i, 0, 0)),
                  pl.BlockSpec((CONV_W, D_MODEL), lambda i: (0, 0)), vec, vec, vec],
        out_specs=pl.BlockSpec((1, n_new, D_MODEL), lambda i: (i, 0, 0)),
        out_shape=jax.ShapeDtypeStruct((b, n_new, D_MODEL), F32),
        compiler_params=_cparams(("arbitrary",)),
        name="conv_sample",
    )(uext, cw, cb, lng, lnb)


def _merge_kernel(x_ref, gt_ref, gpost_ref, zs_ref, at_ref, sga_ref, sgc_ref, wpw_ref, wout_ref, o_ref):
    conv_out = _dot(zs_ref[...].astype(BF16), wpw_ref[...])
    mix = sga_ref[...].astype(F32) * at_ref[...].astype(F32) + sgc_ref[...].astype(F32) * conv_out
    y = _dot(mix.astype(BF16), wout_ref[...])
    o_ref[...] = x_ref[...] + gt_ref[0] * _rms(y, gpost_ref[...])


def _merge(x, gt, gpost, zs, attn, sga, sgc, wpw, wout, *, tm, tiles_per_block):
    t = x.shape[0]
    row = pl.BlockSpec((tm, D_MODEL), lambda i: (i, 0))
    return pl.pallas_call(
        _merge_kernel,
        grid=(t // tm,),
        in_specs=[row, _mod_spec(gt, tiles_per_block), _const_spec((1, D_MODEL)), row, row, row, row,
                  _const_spec(wpw.shape), _const_spec(wout.shape)],
        out_specs=row,
        out_shape=jax.ShapeDtypeStruct((t, D_MODEL), F32),
        compiler_params=_cparams(("arbitrary",)),
        name="merge",
    )(x, gt, gpost, zs, attn, sga, sgc, wpw, wout)


def _decay_kernel(pt_ref, *rest, ppg, page):
    del pt_ref
    lf_refs = rest[0:ppg]
    o_ref = rest[ppg]
    carry_ref = rest[ppg + 1]

    @pl.when(pl.program_id(1) == 0)
    def _():
        carry_ref[...] = jnp.zeros(carry_ref.shape, F32)

    jr = lax.broadcasted_iota(jnp.int32, (page, page), 0)
    ic = lax.broadcasted_iota(jnp.int32, (page, page), 1)
    later = (jr > ic).astype(BF16)
    carry = carry_ref[...]
    for r in range(ppg):
        lft = lf_refs[r][0, 0].T
        parts = jnp.concatenate(_split3(lft), axis=0)
        d3 = _dot(parts, later)
        o_ref[0, ppg - 1 - r] = (d3[0:N_HEADS] + d3[N_HEADS:2 * N_HEADS] + d3[2 * N_HEADS:3 * N_HEADS]) + carry
        carry = carry + jnp.sum(lft, axis=-1, keepdims=True)
    carry_ref[...] = carry


def _decay(layer, page_table, cache_lf, *, ppg):
    b, n_pages = page_table.shape
    page = cache_lf.shape[2]
    groups = n_pages // ppg

    def page_spec(r):
        def imap(s, g, pt):
            return (layer, pt[s * n_pages + (n_pages - 1 - (g * ppg + r))], 0, 0)
        return pl.BlockSpec((1, 1, page, N_HEADS), imap)

    grid_spec = pltpu.PrefetchScalarGridSpec(
        num_scalar_prefetch=1,
        grid=(b, groups),
        in_specs=[page_spec(r) for r in range(ppg)],
        out_specs=pl.BlockSpec((1, ppg, N_HEADS, page), lambda s, g, pt: (s, groups - 1 - g, 0, 0)),
        scratch_shapes=[pltpu.VMEM((N_HEADS, 1), F32)],
    )
    return pl.pallas_call(
        functools.partial(_decay_kernel, ppg=ppg, page=page),
        grid_spec=grid_spec,
        out_shape=jax.ShapeDtypeStruct((b, n_pages, N_HEADS, page), F32),
        compiler_params=_cparams(("arbitrary", "arbitrary")),
        name="decay",
    )(page_table.reshape(-1), *([cache_lf] * ppg))


SUBGROUPS = 4


def _dense_page(ref, tmp_ref, page):
    per = N_HEADS // SUBGROUPS
    for c in range(SUBGROUPS):
        tmp_ref[c] = ref[0, 0, pl.ds(c, page * per, stride=SUBGROUPS), :]
    tiles = []
    for j in range(N_LANE_TILES):
        halves = []
        for s in range(HEADS_PER_LANE_TILE):
            h = HEADS_PER_LANE_TILE * j + s
            halves.append(tmp_ref[h % SUBGROUPS, pl.ds(h // SUBGROUPS, page, stride=per), :])
        tiles.append(jnp.concatenate(halves, axis=1))
    return jnp.concatenate(tiles, axis=1).astype(BF16)


def _attn_sample_kernel(pt_ref, q_ref, kn_ref, vn_ref, lfn_ref, dec_ref, *rest, n_new, ppg, page):
    del pt_ref
    k_refs = rest[0:ppg]
    v_refs = rest[ppg:2 * ppg]
    o_ref = rest[2 * ppg]
    qm_ref, m_ref, l_ref, acc_ref, fn_ref, tmp_ref = rest[2 * ppg + 1:]
    g = pl.program_id(1)
    rows = n_new * N_HEADS
    kpad = kn_ref.shape[1]

    @pl.when(g == 0)
    def _():
        q4 = q_ref[0]
        qrep = jnp.concatenate([jnp.broadcast_to(q4[t:t + 1, :], (N_HEADS, D_MODEL)) for t in range(n_new)], axis=0)
        r = lax.broadcasted_iota(jnp.int32, (rows, D_MODEL), 0)
        c = lax.broadcasted_iota(jnp.int32, (rows, D_MODEL), 1)
        qm = jnp.where((c // HEAD_DIM) == (r % N_HEADS), qrep, 0.0).astype(BF16)
        qm_ref[...] = qm

        er = lax.broadcasted_iota(jnp.int32, (rows, N_HEADS), 0)
        ec = lax.broadcasted_iota(jnp.int32, (rows, N_HEADS), 1)
        expand = ((er % N_HEADS) == ec).astype(F32)
        lfn = lfn_ref[0]
        run = lfn[0:1, :]
        cums = [run]
        for t in range(1, n_new):
            run = run + lfn[t:t + 1, :]
            cums.append(run)
        fn = jnp.concatenate(cums + [jnp.zeros((kpad - n_new, N_HEADS), F32)], axis=0)
        fnt = _dot_nt(expand, fn, precision=HIGHEST)
        trow = lax.broadcasted_iota(jnp.int32, (rows, kpad), 0) // N_HEADS
        tcol = lax.broadcasted_iota(jnp.int32, (rows, kpad), 1)
        fcol = jnp.sum(jnp.where(tcol == trow, fnt, 0.0), axis=-1, keepdims=True)
        fn_ref[...] = fcol
        s = _dot_nt(qm, kn_ref[0]) + (fcol - fnt)
        s = jnp.where(tcol <= trow, s, NEG_INF)
        m = jnp.max(s, axis=-1, keepdims=True)
        p = jnp.exp(s - m)
        m_ref[...] = m
        l_ref[...] = jnp.sum(p, axis=-1, keepdims=True)
        acc_ref[...] = _dot(p.astype(BF16), vn_ref[0])

    qm = qm_ref[...]
    fcol = fn_ref[...]
    kp = jnp.concatenate([_dense_page(k_refs[r], tmp_ref.at[2 * (r % 2)], page) for r in range(ppg)], axis=0)
    vp = jnp.concatenate([_dense_page(v_refs[r], tmp_ref.at[2 * (r % 2) + 1], page) for r in range(ppg)], axis=0)
    decay = jnp.concatenate([jnp.concatenate([dec_ref[0, r]] * n_new, axis=0) for r in range(ppg)], axis=1)
    s = _dot_nt(qm, kp) + (fcol + decay)
    m = m_ref[...]
    m_new = jnp.maximum(m, jnp.max(s, axis=-1, keepdims=True))
    alpha = jnp.exp(m - m_new)
    p = jnp.exp(s - m_new)
    m_ref[...] = m_new
    l_ref[...] = alpha * l_ref[...] + jnp.sum(p, axis=-1, keepdims=True)
    acc_ref[...] = alpha * acc_ref[...] + _dot(p.astype(BF16), vp)

    @pl.when(g == pl.num_programs(1) - 1)
    def _():
        o = acc_ref[...] / l_ref[...]
        r = lax.broadcasted_iota(jnp.int32, (rows, D_MODEL), 0)
        c = lax.broadcasted_iota(jnp.int32, (rows, D_MODEL), 1)
        o = jnp.where((c // HEAD_DIM) == (r % N_HEADS), o, 0.0)
        o_ref[0] = jnp.concatenate(
            [jnp.sum(o[t * N_HEADS:(t + 1) * N_HEADS, :], axis=0, keepdims=True) for t in range(n_new)], axis=0)


def _attn_sample(layer, page_table, q, kn, vn, lfn, decay, cache_k, cache_v, *, ppg):
    b, n_new, _ = q.shape
    n_pages = page_table.shape[1]
    page = cache_k.shape[2] // N_HEADS
    kpad = kn.shape[1]
    rows = n_new * N_HEADS
    groups = n_pages // ppg

    def page_spec(r):
        def imap(s, g, pt):
            return (layer, pt[s * n_pages + g * ppg + r], 0, 0)
        return pl.BlockSpec((1, 1, page * N_HEADS, HEAD_DIM), imap)

    per_seq = lambda nrow, width: pl.BlockSpec((1, nrow, width), lambda s, g, pt: (s, 0, 0))
    in_specs = ([per_seq(n_new, D_MODEL), per_seq(kpad, D_MODEL), per_seq(kpad, D_MODEL), per_seq(n_new, N_HEADS),
                 pl.BlockSpec((1, ppg, N_HEADS, page), lambda s, g, pt: (s, g, 0, 0))]
                + [page_spec(r) for r in range(ppg)] + [page_spec(r) for r in range(ppg)])
    grid_spec = pltpu.PrefetchScalarGridSpec(
        num_scalar_prefetch=1,
        grid=(b, groups),
        in_specs=in_specs,
        out_specs=per_seq(n_new, D_MODEL),
        scratch_shapes=[pltpu.VMEM((rows, D_MODEL), BF16), pltpu.VMEM((rows, 1), F32), pltpu.VMEM((rows, 1), F32),
                        pltpu.VMEM((rows, D_MODEL), F32), pltpu.VMEM((rows, 1), F32),
                        pltpu.VMEM((4, SUBGROUPS, page * N_HEADS // SUBGROUPS, HEAD_DIM), F32)],
    )
    return pl.pallas_call(
        functools.partial(_attn_sample_kernel, n_new=n_new, ppg=ppg, page=page),
        grid_spec=grid_spec,
        out_shape=jax.ShapeDtypeStruct((b, n_new, D_MODEL), F32),
        compiler_params=_cparams(("arbitrary", "arbitrary")),
        name="attn_sample",
    )(page_table.reshape(-1), q, kn, vn, lfn, decay, *([cache_k] * ppg), *([cache_v] * ppg))


def kernel(x_prompt, x_sample, c_prompt, c_sample, cache_k, cache_v, cache_logf, state_conv, page_table,
           norm_pre, norm_post, w_mod, b_mod, ffn1_gate, ffn1_up, ffn1_down, w_in, b_forget,
           conv_w, conv_b, conv_ln_g, conv_ln_b, w_pw, w_out, ffn2_gate, ffn2_up, ffn2_down):
    depth = w_mod.shape[0]
    bp, seq, _ = x_prompt.shape
    bs, n_new, _ = x_sample.shape
    n_phys, page = cache_k.shape[1], cache_k.shape[2]
    tp, ts = bp * seq, bs * n_new
    d_attn = N_HEADS * HEAD_DIM
    assert d_attn == D_MODEL and w_in.shape[2] == 3 * d_attn + N_HEADS + 4 * D_MODEL

    tm_ffn, tm_proj, tq, tm_conv, tm_merge = 512, 256, 256, 256, 512
    assert tm_proj == tq
    kpad = 16

    mod = _modulation(jnp.concatenate([c_prompt, c_sample], axis=0), w_mod, b_mod)
    mod = mod.reshape(depth, bp + bs, N_SUB, 3, D_MODEL)

    ck = cache_k.reshape(depth, n_phys, page * N_HEADS, HEAD_DIM)
    cv = cache_v.reshape(depth, n_phys, page * N_HEADS, HEAD_DIM)

    xp = x_prompt.reshape(tp, D_MODEL)
    xs = x_sample.reshape(ts, D_MODEL)
    outs = [[] for _ in range(8)]
    for l in range(depth):
        wqkv = w_in[l, :, :3 * d_attn].astype(BF16)
        wf = w_in[l, :, 3 * d_attn:3 * d_attn + N_HEADS].astype(BF16)
        wr = w_in[l, :, 3 * d_attn + N_HEADS:].astype(BF16)
        bf = b_forget[l].reshape(1, N_HEADS)
        f1 = (ffn1_gate[l].astype(BF16), ffn1_up[l].astype(BF16), ffn1_down[l].astype(BF16))
        f2 = (ffn2_gate[l].astype(BF16), ffn2_up[l].astype(BF16), ffn2_down[l].astype(BF16))
        wpw = w_pw[l].astype(BF16)
        wout = w_out[l].astype(BF16)
        gpre = [norm_pre[l, s].reshape(1, D_MODEL) for s in range(N_SUB)]
        gpost = [norm_post[l, s].reshape(1, D_MODEL) for s in range(N_SUB)]
        cb = conv_b[l].reshape(1, D_MODEL)
        lng = conv_ln_g[l].reshape(1, D_MODEL)
        lnb = conv_ln_b[l].reshape(1, D_MODEL)
        cw3 = conv_w[l].reshape(CONV_W, N_LANE_TILES, LANES).transpose(1, 0, 2)

        mods_p = [[mod[l, :bp, s, j].reshape(bp, 1, D_MODEL) for j in range(3)] for s in range(N_SUB)]
        mods_s = [[jnp.repeat(mod[l, bp:, s, j], n_new, axis=0).reshape(1, ts, D_MODEL) for j in range(3)]
                  for s in range(N_SUB)]

        xp = _ffn(xp, mods_p[0], gpre[0], gpost[0], *f1, tm=tm_ffn, tiles_per_block=seq // tm_ffn, res_w=0.5)
        q, k, v, kb, _, vt, lf, fc, u, sga, sgc = _proj(
            xp, mods_p[1], gpre[1], wqkv, wf, wr, bf, tm=tm_proj, tiles_per_block=seq // tm_proj,
            tiles_per_seq=seq // tm_proj, q_scale=QK_SCALE * LOG2E)
        fcol = fc.reshape(bp, seq, N_HEADS)
        attn = _attn_prompt(q.reshape(bp, seq, D_MODEL), kb.reshape(bp, seq, D_MODEL),
                            vt.reshape(bp, seq // tq, D_MODEL, tq), fcol.transpose(0, 2, 1), fcol, tq=tq)
        u3 = u.reshape(bp, seq, D_MODEL)
        zs = _conv_prompt(u3, cw3, cb, lng, lnb, tm=tm_conv)
        xp = _merge(xp, mods_p[1][2], gpost[1], zs.reshape(tp, D_MODEL), attn.reshape(tp, D_MODEL), sga, sgc,
                    wpw, wout, tm=tm_merge, tiles_per_block=seq // tm_merge)
        xp = _ffn(xp, mods_p[2], gpre[2], gpost[2], *f2, tm=tm_ffn, tiles_per_block=seq // tm_ffn, res_w=0.5)
        outs[0].append(k.reshape(bp, seq, N_HEADS, HEAD_DIM))
        outs[1].append(v.reshape(bp, seq, N_HEADS, HEAD_DIM))
        outs[2].append(lf.reshape(bp, seq, N_HEADS))
        outs[3].append(u3[:, seq - (CONV_W - 1):, :])

        xs = _ffn(xs, mods_s[0], gpre[0], gpost[0], *f1, tm=ts, tiles_per_block=1, res_w=0.5)
        q, k, v, kb, vb, _, lf, _, u, sga, sgc = _proj(
            xs, mods_s[1], gpre[1], wqkv, wf, wr, bf, tm=ts, tiles_per_block=1, tiles_per_seq=1, q_scale=QK_SCALE)
        pad = lambda a: jnp.pad(a.reshape(bs, n_new, D_MODEL), ((0, 0), (0, kpad - n_new), (0, 0)))
        decay = _decay(l, page_table, cache_logf, ppg=8)
        attn = _attn_sample(l, page_table, (q.astype(F32)).reshape(bs, n_new, D_MODEL), pad(kb), pad(vb),
                            lf.reshape(bs, n_new, N_HEADS), decay, ck, cv, ppg=8)
        uext = jnp.concatenate([state_conv[l], u.reshape(bs, n_new, D_MODEL)], axis=1)
        zs = _conv_sample(uext, conv_w[l], cb, lng, lnb, n_new=n_new)
        xs = _merge(xs, mods_s[1][2], gpost[1], zs.reshape(ts, D_MODEL), attn.reshape(ts, D_MODEL), sga, sgc,
                    wpw, wout, tm=ts, tiles_per_block=1)
        xs = _ffn(xs, mods_s[2], gpre[2], gpost[2], *f2, tm=ts, tiles_per_block=1, res_w=0.5)
        outs[4].append(k.reshape(bs, n_new, N_HEADS, HEAD_DIM))
        outs[5].append(v.reshape(bs, n_new, N_HEADS, HEAD_DIM))
        outs[6].append(lf.reshape(bs, n_new, N_HEADS))
        outs[7].append(uext[:, n_new:, :])

    stacked = [jnp.stack(o, axis=0) for o in outs]
    return (xp.reshape(bp, seq, D_MODEL), xs.reshape(bs, n_new, D_MODEL), *stacked)
```

```python
import functools

import jax
import jax.numpy as jnp
from jax import lax
from jax.experimental import pallas as pl
from jax.experimental.pallas import tpu as pltpu

F32 = jnp.float32
BF16 = jnp.bfloat16
HIGHEST = lax.Precision.HIGHEST

D_MODEL = 1024
N_HEADS = 16
HEAD_DIM = 64
D_FF = 2816
CONV_W = 31
N_SUB = 3
EPS = 1e-6
NEG_INF = -1e30
QK_SCALE = HEAD_DIM ** -0.5
LOG2E = 1.4426950408889634
LANES = 128
HEADS_PER_LANE_TILE = LANES // HEAD_DIM
N_LANE_TILES = D_MODEL // LANES
CONV_HALO = 32

VMEM_LIMIT = 56 * 1024 * 1024


def _cparams(sem):
    return pltpu.CompilerParams(dimension_semantics=sem, vmem_limit_bytes=VMEM_LIMIT)


def _const_spec(shape):
    nd = len(shape)
    return pl.BlockSpec(shape, lambda *_: (0,) * nd, pipeline_mode=pl.Buffered(1))


def _rms(x, g):
    return x * lax.rsqrt(jnp.mean(x * x, axis=-1, keepdims=True) + EPS) * g


def _dot(a, b, **kw):
    return jnp.dot(a, b, preferred_element_type=F32, **kw)


def _dot_nt(a, b, **kw):
    return lax.dot_general(a, b, (((1,), (1,)), ((), ())), preferred_element_type=F32, **kw)


def _split3(x):
    hi = x.astype(BF16)
    r = x - hi.astype(F32)
    mid = r.astype(BF16)
    lo = (r - mid.astype(F32)).astype(BF16)
    return hi, mid, lo


def _mod_kernel(c_ref, w_ref, b_ref, o_ref):
    c = c_ref[...]
    a = (c * jax.nn.sigmoid(c)).astype(BF16)
    o_ref[0] = _dot(a, w_ref[0].astype(BF16)) + b_ref[0]


def _modulation(c_all, w_mod, b_mod):
    depth, _, n = w_mod.shape
    rows = c_all.shape[0]
    tn = 1152
    return pl.pallas_call(
        _mod_kernel,
        grid=(depth, n // tn),
        in_specs=[
            pl.BlockSpec((rows, D_MODEL), lambda l, j: (0, 0)),
            pl.BlockSpec((1, D_MODEL, tn), lambda l, j: (l, 0, j)),
            pl.BlockSpec((1, 1, tn), lambda l, j: (l, 0, j)),
        ],
        out_specs=pl.BlockSpec((1, rows, tn), lambda l, j: (l, 0, j)),
        out_shape=jax.ShapeDtypeStruct((depth, rows, n), F32),
        compiler_params=_cparams(("arbitrary", "arbitrary")),
        name="modulation",
    )(c_all, w_mod, b_mod.reshape(depth, 1, n))


def _mod_spec(m, tiles_per_block):
    rows = m.shape[1]
    return pl.BlockSpec((1, rows, D_MODEL), lambda i: (i // tiles_per_block, 0, 0))


def _ffn_kernel(x_ref, sh_ref, sc_ref, gt_ref, gpre_ref, gpost_ref, wg_ref, wu_ref, wd_ref, o_ref, *, res_w, ck):
    x = x_ref[...]
    h = (_rms(x, gpre_ref[...]) * (1.0 + sc_ref[0]) + sh_ref[0]).astype(BF16)
    acc = jnp.zeros(x.shape, F32)
    for c in range(D_FF // ck):
        g = _dot(h, wg_ref[:, c * ck:(c + 1) * ck])
        u = _dot(h, wu_ref[:, c * ck:(c + 1) * ck])
        a = (g * jax.nn.sigmoid(g) * u).astype(BF16)
        acc = acc + _dot(a, wd_ref[c * ck:(c + 1) * ck, :])
    o_ref[...] = x + res_w * gt_ref[0] * _rms(acc, gpost_ref[...])


def _ffn(x, mods, gpre, gpost, wg, wu, wd, *, tm, tiles_per_block, res_w):
    t = x.shape[0]
    sh, sc, gt = mods
    row = pl.BlockSpec((tm, D_MODEL), lambda i: (i, 0))
    return pl.pallas_call(
        functools.partial(_ffn_kernel, res_w=res_w, ck=D_FF),
        grid=(t // tm,),
        in_specs=[row, _mod_spec(sh, tiles_per_block), _mod_spec(sc, tiles_per_block),
                  _mod_spec(gt, tiles_per_block), _const_spec((1, D_MODEL)), _const_spec((1, D_MODEL)),
                  _const_spec(wg.shape), _const_spec(wu.shape), _const_spec(wd.shape)],
        out_specs=row,
        out_shape=jax.ShapeDtypeStruct((t, D_MODEL), F32),
        compiler_params=_cparams(("arbitrary",)),
        name="ffn",
    )(x, sh, sc, gt, gpre, gpost, wg, wu, wd)


def _proj_kernel(x_ref, sh_ref, sc_ref, gpre_ref, wqkv_ref, wf_ref, wr_ref, bf_ref, *rest,
                 tiles_per_seq, q_scale, n_aliased):
    q_ref, k_ref, v_ref, kb_ref, vb_ref, vt_ref, lf_ref, fc_ref, u_ref, sga_ref, sgc_ref, carry_ref = rest[n_aliased:]
    x = x_ref[...]
    tm = x.shape[0]
    h = (_rms(x, gpre_ref[...]) * (1.0 + sc_ref[0]) + sh_ref[0]).astype(BF16)
    q_ref[...] = (_dot(h, wqkv_ref[:, 0:D_MODEL]) * q_scale).astype(BF16)
    k = _dot(h, wqkv_ref[:, D_MODEL:2 * D_MODEL])
    k_ref[0] = k
    kb_ref[...] = k.astype(BF16)
    v = _dot(h, wqkv_ref[:, 2 * D_MODEL:3 * D_MODEL])
    v_ref[0] = v
    vb_ref[...] = v.astype(BF16)
    vt_ref[0] = v.T.astype(BF16)

    f = _dot(h, wf_ref[...]) + bf_ref[...]
    lf = jnp.minimum(f, 0.0) - jnp.log1p(jnp.exp(-jnp.abs(f)))
    lf_ref[...] = lf

    @pl.when(pl.program_id(0) % tiles_per_seq == 0)
    def _():
        carry_ref[...] = jnp.zeros(carry_ref.shape, F32)

    r = lax.broadcasted_iota(jnp.int32, (tm, tm), 0)
    c = lax.broadcasted_iota(jnp.int32, (tm, tm), 1)
    tri = (c <= r).astype(F32)
    fc = _dot(tri, lf, precision=HIGHEST) + carry_ref[...]
    fc_ref[...] = fc
    carry_ref[...] = fc[tm - 1:tm, :]

    a = _dot(h, wr_ref[:, 0:D_MODEL])
    b = _dot(h, wr_ref[:, D_MODEL:2 * D_MODEL])
    u_ref[...] = a * jax.nn.sigmoid(b)
    sga_ref[...] = jax.nn.sigmoid(_dot(h, wr_ref[:, 2 * D_MODEL:3 * D_MODEL])).astype(BF16)
    sgc_ref[...] = jax.nn.sigmoid(_dot(h, wr_ref[:, 3 * D_MODEL:4 * D_MODEL])).astype(BF16)


def _proj(x, mods, gpre, wqkv, wf, wr, bf, *, tm, tiles_per_block, tiles_per_seq, q_scale, layer, depth, kv_stack):
    t = x.shape[0]
    sh, sc, _ = mods
    row = pl.BlockSpec((tm, D_MODEL), lambda i: (i, 0))
    lrow = pl.BlockSpec((1, tm, D_MODEL), lambda i: (layer, i, 0))
    stack = jax.ShapeDtypeStruct((depth, t, D_MODEL), F32)
    n_fixed = 8
    hrow = pl.BlockSpec((tm, N_HEADS), lambda i: (i, 0))
    trow = pl.BlockSpec((1, D_MODEL, tm), lambda i: (i, 0, 0))
    wide = lambda dt: jax.ShapeDtypeStruct((t, D_MODEL), dt)
    narrow = jax.ShapeDtypeStruct((t, N_HEADS), F32)
    return pl.pallas_call(
        functools.partial(_proj_kernel, tiles_per_seq=tiles_per_seq, q_scale=q_scale, n_aliased=len(kv_stack)),
        grid=(t // tm,),
        in_specs=[row, _mod_spec(sh, tiles_per_block), _mod_spec(sc, tiles_per_block),
                  _const_spec((1, D_MODEL)), _const_spec(wqkv.shape), _const_spec(wf.shape),
                  _const_spec(wr.shape), _const_spec((1, N_HEADS))]
        + [pl.BlockSpec(memory_space=pl.ANY) for _ in kv_stack],
        out_specs=[row, lrow, lrow, row, row, trow, hrow, hrow, row, row, row],
        out_shape=[wide(BF16), stack, stack, wide(BF16), wide(BF16),
                   jax.ShapeDtypeStruct((t // tm, D_MODEL, tm), BF16), narrow, narrow,
                   wide(F32), wide(BF16), wide(BF16)],
        scratch_shapes=[pltpu.VMEM((1, N_HEADS), F32)],
        input_output_aliases={n_fixed + j: 1 + j for j in range(len(kv_stack))},
        compiler_params=_cparams(("arbitrary",)),
        name="proj",
    )(x, sh, sc, gpre, wqkv, wf, wr, bf, *kv_stack)


def _attn_prompt_kernel(q_ref, k_ref, vt_ref, frow_ref, fcol_ref, o_ref, qm_ref, m_ref, l_ref, acc_ref, *s_refs, tq):
    qi = pl.program_id(1)
    lane = lax.broadcasted_iota(jnp.int32, (tq, LANES), 1)
    first = lane < HEAD_DIM
    key_idx = lax.broadcasted_iota(jnp.int32, (tq, tq), 0)
    qry_idx = lax.broadcasted_iota(jnp.int32, (tq, tq), 1)
    causal = key_idx <= qry_idx

    for hp in range(N_LANE_TILES):
        qp = q_ref[0, :, hp * LANES:(hp + 1) * LANES]
        zero = jnp.zeros_like(qp)
        qm_ref[HEADS_PER_LANE_TILE * hp] = jnp.where(first, qp, zero)
        qm_ref[HEADS_PER_LANE_TILE * hp + 1] = jnp.where(first, zero, qp)
    m_ref[...] = jnp.full(m_ref.shape, NEG_INF, F32)
    l_ref[...] = jnp.zeros(l_ref.shape, F32)
    acc_ref[...] = jnp.zeros(acc_ref.shape, F32)

    def scores(j):
        start = pl.multiple_of(j * tq, tq)
        for h in range(N_HEADS):
            hp = h // HEADS_PER_LANE_TILE
            kb = k_ref[0, pl.ds(start, tq), hp * LANES:(hp + 1) * LANES]
            s_refs[h][...] = _dot_nt(kb, qm_ref[h])

    def softmax_pv(j, masked):
        start = pl.multiple_of(j * tq, tq)
        for h in range(N_HEADS):
            f_i = frow_ref[0, h:h + 1, :] * LOG2E
            f_j = fcol_ref[0, pl.ds(start, tq), h:h + 1] * LOG2E
            st = s_refs[h][...] + (f_i - f_j)
            if masked:
                st = jnp.where(causal, st, NEG_INF)
            m = m_ref[h]
            m_new = jnp.maximum(m, jnp.max(st, axis=0, keepdims=True))
            alpha = jnp.exp2(m - m_new)
            p = jnp.exp2(st - m_new)
            m_ref[h] = m_new
            l_ref[h] = alpha * l_ref[h] + jnp.sum(p, axis=0, keepdims=True)
            vt = vt_ref[0, j, h * HEAD_DIM:(h + 1) * HEAD_DIM, :]
            acc_ref[h] = alpha * acc_ref[h] + _dot(vt, p.astype(BF16))

    def body(j, c):
        softmax_pv(j - 1, False)
        scores(j)
        return c

    scores(0)
    lax.fori_loop(1, qi + 1, body, 0)
    softmax_pv(qi, True)
    for hp in range(N_LANE_TILES):
        h0 = HEADS_PER_LANE_TILE * hp
        ot = jnp.concatenate([acc_ref[h0] / l_ref[h0], acc_ref[h0 + 1] / l_ref[h0 + 1]], axis=0)
        o_ref[0, :, hp * LANES:(hp + 1) * LANES] = ot.T.astype(o_ref.dtype)


def _attn_prompt(q, kb, vt, frow, fcol, *, tq):
    b, t, _ = q.shape
    nk = t // tq
    return pl.pallas_call(
        functools.partial(_attn_prompt_kernel, tq=tq),
        grid=(b, nk),
        in_specs=[
            pl.BlockSpec((1, tq, D_MODEL), lambda bi, i: (bi, i, 0)),
            pl.BlockSpec((1, t, D_MODEL), lambda bi, i: (bi, 0, 0)),
            pl.BlockSpec((1, nk, D_MODEL, tq), lambda bi, i: (bi, 0, 0, 0)),
            pl.BlockSpec((1, N_HEADS, tq), lambda bi, i: (bi, 0, i)),
            pl.BlockSpec((1, t, N_HEADS), lambda bi, i: (bi, 0, 0)),
        ],
        out_specs=pl.BlockSpec((1, tq, D_MODEL), lambda bi, i: (bi, i, 0)),
        out_shape=jax.ShapeDtypeStruct((b, t, D_MODEL), BF16),
        scratch_shapes=[pltpu.VMEM((N_HEADS, tq, LANES), BF16), pltpu.VMEM((N_HEADS, 1, tq), F32),
                        pltpu.VMEM((N_HEADS, 1, tq), F32), pltpu.VMEM((N_HEADS, HEAD_DIM, tq), F32)]
        + [pltpu.VMEM((tq, tq), F32) for _ in range(N_HEADS)],
        compiler_params=_cparams(("arbitrary", "arbitrary")),
        name="attn_prompt",
    )(q, kb, vt, frow, fcol)


def _layer_norm_swish(z, g, b):
    mu = jnp.mean(z, axis=-1, keepdims=True)
    zc = z - mu
    var = jnp.mean(zc * zc, axis=-1, keepdims=True)
    y = zc * lax.rsqrt(var + EPS) * g + b
    return y * jax.nn.sigmoid(y)


def _conv_prompt_kernel(u_ref, cw_ref, cb_ref, lng_ref, lnb_ref, z_ref, uext_ref, zs_ref, *, tm, rchunk):
    ti = pl.program_id(1)
    for cbk in range(N_LANE_TILES):
        @pl.when(ti == 0)
        def _():
            uext_ref[cbk, 0:CONV_HALO, :] = jnp.zeros((CONV_HALO, LANES), F32)

        @pl.when(ti > 0)
        def _():
            uext_ref[cbk, 0:CONV_HALO, :] = uext_ref[cbk, tm:tm + CONV_HALO, :]

    for cbk in range(N_LANE_TILES):
        uext_ref[cbk, CONV_HALO:CONV_HALO + tm, :] = u_ref[0, :, cbk * LANES:(cbk + 1) * LANES]

    first_tap = CONV_HALO - (CONV_W - 1)

    def body(cbk, _):
        w = cw_ref[cbk]
        for rb in range(tm // rchunk):
            acc = jnp.zeros((rchunk, LANES), F32)
            for j in range(CONV_W):
                start = rb * rchunk + first_tap + j
                acc = acc + w[j:j + 1, :] * uext_ref[cbk, pl.ds(start, rchunk), :]
            zs_ref[cbk, rb * rchunk:(rb + 1) * rchunk, :] = acc
        return 0

    lax.fori_loop(0, N_LANE_TILES, body, 0)
    z = jnp.concatenate([zs_ref[cbk] for cbk in range(N_LANE_TILES)], axis=-1) + cb_ref[...]
    z_ref[0] = _layer_norm_swish(z, lng_ref[...], lnb_ref[...]).astype(z_ref.dtype)


def _conv_prompt(u, cw3, cb, lng, lnb, *, tm):
    b, t, _ = u.shape
    row = pl.BlockSpec((1, tm, D_MODEL), lambda bi, i: (bi, i, 0))
    return pl.pallas_call(
        functools.partial(_conv_prompt_kernel, tm=tm, rchunk=64),
        grid=(b, t // tm),
        in_specs=[row, pl.BlockSpec(cw3.shape, lambda bi, i: (0, 0, 0)),
                  pl.BlockSpec((1, D_MODEL), lambda bi, i: (0, 0)),
                  pl.BlockSpec((1, D_MODEL), lambda bi, i: (0, 0)),
                  pl.BlockSpec((1, D_MODEL), lambda bi, i: (0, 0))],
        out_specs=row,
        out_shape=jax.ShapeDtypeStruct((b, t, D_MODEL), BF16),
        scratch_shapes=[pltpu.VMEM((N_LANE_TILES, CONV_HALO + tm, LANES), F32),
                        pltpu.VMEM((N_LANE_TILES, tm, LANES), F32)],
        compiler_params=_cparams(("arbitrary", "arbitrary")),
        name="conv_prompt",
    )(u, cw3, cb, lng, lnb)


def _conv_sample_kernel(uext_ref, cw_ref, cb_ref, lng_ref, lnb_ref, z_ref, *, n_new):
    acc = jnp.zeros((n_new, D_MODEL), F32)
    for j in range(CONV_W):
        acc = acc + cw_ref[j:j + 1, :] * uext_ref[0, j:j + n_new, :]
    z_ref[0] = _layer_norm_swish(acc + cb_ref[...], lng_ref[...], lnb_ref[...])


def _conv_sample(uext, cw, cb, lng, lnb, *, n_new):
    b, rows, _ = uext.shape
    vec = pl.BlockSpec((1, D_MODEL), lambda i: (0, 0))
    return pl.pallas_call(
        functools.partial(_conv_sample_kernel, n_new=n_new),
        grid=(b,),
        in_specs=[pl.BlockSpec((1, rows, D_MODEL), lambda i: (i, 0, 0)),
                  pl.BlockSpec((CONV_W, D_MODEL), lambda i: (0, 0)), vec, vec, vec],
        out_specs=pl.BlockSpec((1, n_new, D_MODEL), lambda i: (i, 0, 0)),
        out_shape=jax.ShapeDtypeStruct((b, n_new, D_MODEL), F32),
        compiler_params=_cparams(("arbitrary",)),
        name="conv_sample",
    )(uext, cw, cb, lng, lnb)


def _merge_kernel(x_ref, gt_ref, gpost_ref, zs_ref, at_ref, sga_ref, sgc_ref, wpw_ref, wout_ref, o_ref):
    conv_out = _dot(zs_ref[...].astype(BF16), wpw_ref[...])
    mix = sga_ref[...].astype(F32) * at_ref[...].astype(F32) + sgc_ref[...].astype(F32) * conv_out
    y = _dot(mix.astype(BF16), wout_ref[...])
    o_ref[...] = x_ref[...] + gt_ref[0] * _rms(y, gpost_ref[...])


def _merge(x, gt, gpost, zs, attn, sga, sgc, wpw, wout, *, tm, tiles_per_block):
    t = x.shape[0]
    row = pl.BlockSpec((tm, D_MODEL), lambda i: (i, 0))
    return pl.pallas_call(
        _merge_kernel,
        grid=(t // tm,),
        in_specs=[row, _mod_spec(gt, tiles_per_block), _const_spec((1, D_MODEL)), row, row, row, row,
                  _const_spec(wpw.shape), _const_spec(wout.shape)],
        out_specs=row,
        out_shape=jax.ShapeDtypeStruct((t, D_MODEL), F32),
        compiler_params=_cparams(("arbitrary",)),
        name="merge",
    )(x, gt, gpost, zs, attn, sga, sgc, wpw, wout)


def _page_decays(lf_refs, carry, page):
    jr = lax.broadcasted_iota(jnp.int32, (page, page), 0)
    ic = lax.broadcasted_iota(jnp.int32, (page, page), 1)
    later = (jr > ic).astype(BF16)
    lfts = [ref[0, 0].T for ref in lf_refs]
    parts = jnp.concatenate([part for lft in lfts for part in _split3(lft)], axis=0)
    d3 = _dot(parts, later)
    decays = []
    for r, lft in enumerate(lfts):
        lo = 3 * N_HEADS * r
        within = d3[lo:lo + N_HEADS] + d3[lo + N_HEADS:lo + 2 * N_HEADS] + d3[lo + 2 * N_HEADS:lo + 3 * N_HEADS]
        decays.append(within + carry)
        carry = carry + jnp.sum(lft, axis=-1, keepdims=True)
    return decays, carry


SUBGROUPS = 4


def _dense_page(ref, tmp_ref, page):
    per = N_HEADS // SUBGROUPS
    flat = ref.at[0, 0].reshape(page * N_HEADS, HEAD_DIM)
    for c in range(SUBGROUPS):
        tmp_ref[c] = flat[pl.ds(c, page * per, stride=SUBGROUPS), :]
    tiles = []
    for j in range(N_LANE_TILES):
        halves = []
        for s in range(HEADS_PER_LANE_TILE):
            h = HEADS_PER_LANE_TILE * j + s
            halves.append(tmp_ref[h % SUBGROUPS, pl.ds(h // SUBGROUPS, page, stride=per), :])
        tiles.append(jnp.concatenate(halves, axis=1))
    return jnp.concatenate(tiles, axis=1).astype(BF16)


def _attn_sample_kernel(pt_ref, q_ref, kn_ref, vn_ref, lfn_ref, *rest, n_new, ppg, page):
    del pt_ref
    k_refs = rest[0:ppg]
    v_refs = rest[ppg:2 * ppg]
    lf_refs = rest[2 * ppg:3 * ppg]
    o_ref = rest[3 * ppg]
    qm_ref, m_ref, l_ref, acc_ref, fn_ref, carry_ref, tmp_ref = rest[3 * ppg + 1:]
    g = pl.program_id(1)
    rows = n_new * N_HEADS
    kpad = kn_ref.shape[1]

    @pl.when(g == 0)
    def _():
        q4 = q_ref[0]
        qrep = jnp.concatenate([jnp.broadcast_to(q4[t:t + 1, :], (N_HEADS, D_MODEL)) for t in range(n_new)], axis=0)
        r = lax.broadcasted_iota(jnp.int32, (rows, D_MODEL), 0)
        c = lax.broadcasted_iota(jnp.int32, (rows, D_MODEL), 1)
        qm = jnp.where((c // HEAD_DIM) == (r % N_HEADS), qrep, 0.0).astype(BF16)
        qm_ref[...] = qm

        er = lax.broadcasted_iota(jnp.int32, (rows, N_HEADS), 0)
        ec = lax.broadcasted_iota(jnp.int32, (rows, N_HEADS), 1)
        expand = ((er % N_HEADS) == ec).astype(F32)
        lfn = lfn_ref[0]
        run = lfn[0:1, :]
        cums = [run]
        for t in range(1, n_new):
            run = run + lfn[t:t + 1, :]
            cums.append(run)
        fn = jnp.concatenate(cums + [jnp.zeros((kpad - n_new, N_HEADS), F32)], axis=0)
        fnt = _dot_nt(expand, fn, precision=HIGHEST)
        trow = lax.broadcasted_iota(jnp.int32, (rows, kpad), 0) // N_HEADS
        tcol = lax.broadcasted_iota(jnp.int32, (rows, kpad), 1)
        fcol = jnp.sum(jnp.where(tcol == trow, fnt, 0.0), axis=-1, keepdims=True)
        fn_ref[...] = fcol
        s = _dot_nt(qm, kn_ref[0]) + (fcol - fnt)
        s = jnp.where(tcol <= trow, s, NEG_INF)
        m = jnp.max(s, axis=-1, keepdims=True)
        p = jnp.exp(s - m)
        m_ref[...] = m
        l_ref[...] = jnp.sum(p, axis=-1, keepdims=True)
        acc_ref[...] = _dot(p.astype(BF16), vn_ref[0])
        carry_ref[...] = jnp.zeros(carry_ref.shape, F32)

    qm = qm_ref[...]
    fcol = fn_ref[...]
    kp = jnp.concatenate([_dense_page(k_refs[r], tmp_ref.at[2 * (r % 2)], page) for r in range(ppg)], axis=0)
    vp = jnp.concatenate([_dense_page(v_refs[r], tmp_ref.at[2 * (r % 2) + 1], page) for r in range(ppg)], axis=0)
    decays, carry = _page_decays(lf_refs, carry_ref[...], page)
    carry_ref[...] = carry
    decay = jnp.concatenate([jnp.concatenate([d] * n_new, axis=0) for d in decays], axis=1)
    s = _dot_nt(qm, kp) + (fcol + decay)
    m = m_ref[...]
    m_new = jnp.maximum(m, jnp.max(s, axis=-1, keepdims=True))
    alpha = jnp.exp(m - m_new)
    p = jnp.exp(s - m_new)
    m_ref[...] = m_new
    l_ref[...] = alpha * l_ref[...] + jnp.sum(p, axis=-1, keepdims=True)
    acc_ref[...] = alpha * acc_ref[...] + _dot(p.astype(BF16), vp)

    @pl.when(g == pl.num_programs(1) - 1)
    def _():
        o = acc_ref[...] / l_ref[...]
        r = lax.broadcasted_iota(jnp.int32, (rows, D_MODEL), 0)
        c = lax.broadcasted_iota(jnp.int32, (rows, D_MODEL), 1)
        o = jnp.where((c // HEAD_DIM) == (r % N_HEADS), o, 0.0)
        o_ref[0] = jnp.concatenate(
            [jnp.sum(o[t * N_HEADS:(t + 1) * N_HEADS, :], axis=0, keepdims=True) for t in range(n_new)], axis=0)


def _attn_sample(layer, page_table, q, kn, vn, lfn, cache_k, cache_v, cache_lf, *, ppg):
    b, n_new, _ = q.shape
    n_pages = page_table.shape[1]
    page = cache_k.shape[2]
    kpad = kn.shape[1]
    rows = n_new * N_HEADS
    groups = n_pages // ppg

    def page_spec(r, tail):
        def imap(s, g, pt):
            return (layer, pt[s * n_pages + (n_pages - 1 - (g * ppg + r))], 0) + (0,) * len(tail)
        return pl.BlockSpec((1, 1, page) + tail, imap)

    per_seq = lambda nrow, width: pl.BlockSpec((1, nrow, width), lambda s, g, pt: (s, 0, 0))
    in_specs = ([per_seq(n_new, D_MODEL), per_seq(kpad, D_MODEL), per_seq(kpad, D_MODEL), per_seq(n_new, N_HEADS)]
                + [page_spec(r, (N_HEADS, HEAD_DIM)) for r in range(ppg)]
                + [page_spec(r, (N_HEADS, HEAD_DIM)) for r in range(ppg)]
                + [page_spec(r, (N_HEADS,)) for r in range(ppg)])
    grid_spec = pltpu.PrefetchScalarGridSpec(
        num_scalar_prefetch=1,
        grid=(b, groups),
        in_specs=in_specs,
        out_specs=per_seq(n_new, D_MODEL),
        scratch_shapes=[pltpu.VMEM((rows, D_MODEL), BF16), pltpu.VMEM((rows, 1), F32), pltpu.VMEM((rows, 1), F32),
                        pltpu.VMEM((rows, D_MODEL), F32), pltpu.VMEM((rows, 1), F32), pltpu.VMEM((N_HEADS, 1), F32),
                        pltpu.VMEM((4, SUBGROUPS, page * N_HEADS // SUBGROUPS, HEAD_DIM), F32)],
    )
    return pl.pallas_call(
        functools.partial(_attn_sample_kernel, n_new=n_new, ppg=ppg, page=page),
        grid_spec=grid_spec,
        out_shape=jax.ShapeDtypeStruct((b, n_new, D_MODEL), F32),
        compiler_params=_cparams(("arbitrary", "arbitrary")),
        name="attn_sample",
    )(page_table.reshape(-1), q, kn, vn, lfn, *([cache_k] * ppg), *([cache_v] * ppg), *([cache_lf] * ppg))


def kernel(x_prompt, x_sample, c_prompt, c_sample, cache_k, cache_v, cache_logf, state_conv, page_table,
           norm_pre, norm_post, w_mod, b_mod, ffn1_gate, ffn1_up, ffn1_down, w_in, b_forget,
           conv_w, conv_b, conv_ln_g, conv_ln_b, w_pw, w_out, ffn2_gate, ffn2_up, ffn2_down):
    depth = w_mod.shape[0]
    bp, seq, _ = x_prompt.shape
    bs, n_new, _ = x_sample.shape
    n_phys, page = cache_k.shape[1], cache_k.shape[2]
    tp, ts = bp * seq, bs * n_new
    d_attn = N_HEADS * HEAD_DIM
    assert d_attn == D_MODEL and w_in.shape[2] == 3 * d_attn + N_HEADS + 4 * D_MODEL

    tm_ffn, tm_proj, tq, tm_conv, tm_merge = 512, 256, 256, 256, 512
    assert tm_proj == tq
    kpad = 16

    mod = _modulation(jnp.concatenate([c_prompt, c_sample], axis=0), w_mod, b_mod)
    mod = mod.reshape(depth, bp + bs, N_SUB, 3, D_MODEL)

    xp = x_prompt.reshape(tp, D_MODEL)
    xs = x_sample.reshape(ts, D_MODEL)
    outs = [[] for _ in range(8)]
    kv_p, kv_s = (), ()
    for l in range(depth):
        wqkv = w_in[l, :, :3 * d_attn].astype(BF16)
        wf = w_in[l, :, 3 * d_attn:3 * d_attn + N_HEADS].astype(BF16)
        wr = w_in[l, :, 3 * d_attn + N_HEADS:].astype(BF16)
        bf = b_forget[l].reshape(1, N_HEADS)
        f1 = (ffn1_gate[l].astype(BF16), ffn1_up[l].astype(BF16), ffn1_down[l].astype(BF16))
        f2 = (ffn2_gate[l].astype(BF16), ffn2_up[l].astype(BF16), ffn2_down[l].astype(BF16))
        wpw = w_pw[l].astype(BF16)
        wout = w_out[l].astype(BF16)
        gpre = [norm_pre[l, s].reshape(1, D_MODEL) for s in range(N_SUB)]
        gpost = [norm_post[l, s].reshape(1, D_MODEL) for s in range(N_SUB)]
        cb = conv_b[l].reshape(1, D_MODEL)
        lng = conv_ln_g[l].reshape(1, D_MODEL)
        lnb = conv_ln_b[l].reshape(1, D_MODEL)
        cw3 = conv_w[l].reshape(CONV_W, N_LANE_TILES, LANES).transpose(1, 0, 2)

        mods_p = [[mod[l, :bp, s, j].reshape(bp, 1, D_MODEL) for j in range(3)] for s in range(N_SUB)]
        mods_s = [[jnp.repeat(mod[l, bp:, s, j], n_new, axis=0).reshape(1, ts, D_MODEL) for j in range(3)]
                  for s in range(N_SUB)]

        xp = _ffn(xp, mods_p[0], gpre[0], gpost[0], *f1, tm=tm_ffn, tiles_per_block=seq // tm_ffn, res_w=0.5)
        q, k, v, kb, _, vt, lf, fc, u, sga, sgc = _proj(
            xp, mods_p[1], gpre[1], wqkv, wf, wr, bf, tm=tm_proj, tiles_per_block=seq // tm_proj,
            tiles_per_seq=seq // tm_proj, q_scale=QK_SCALE * LOG2E, layer=l, depth=depth, kv_stack=kv_p)
        kv_p = (k, v)
        fcol = fc.reshape(bp, seq, N_HEADS)
        attn = _attn_prompt(q.reshape(bp, seq, D_MODEL), kb.reshape(bp, seq, D_MODEL),
                            vt.reshape(bp, seq // tq, D_MODEL, tq), fcol.transpose(0, 2, 1), fcol, tq=tq)
        u3 = u.reshape(bp, seq, D_MODEL)
        zs = _conv_prompt(u3, cw3, cb, lng, lnb, tm=tm_conv)
        xp = _merge(xp, mods_p[1][2], gpost[1], zs.reshape(tp, D_MODEL), attn.reshape(tp, D_MODEL), sga, sgc,
                    wpw, wout, tm=tm_merge, tiles_per_block=seq // tm_merge)
        xp = _ffn(xp, mods_p[2], gpre[2], gpost[2], *f2, tm=tm_ffn, tiles_per_block=seq // tm_ffn, res_w=0.5)
        outs[2].append(lf.reshape(bp, seq, N_HEADS))
        outs[3].append(u3[:, seq - (CONV_W - 1):, :])

        xs = _ffn(xs, mods_s[0], gpre[0], gpost[0], *f1, tm=ts, tiles_per_block=1, res_w=0.5)
        q, k, v, kb, vb, _, lf, _, u, sga, sgc = _proj(
            xs, mods_s[1], gpre[1], wqkv, wf, wr, bf, tm=ts, tiles_per_block=1, tiles_per_seq=1, q_scale=QK_SCALE,
            layer=l, depth=depth, kv_stack=kv_s)
        kv_s = (k, v)
        pad = lambda a: jnp.pad(a.reshape(bs, n_new, D_MODEL), ((0, 0), (0, kpad - n_new), (0, 0)))
        attn = _attn_sample(l, page_table, (q.astype(F32)).reshape(bs, n_new, D_MODEL), pad(kb), pad(vb),
                            lf.reshape(bs, n_new, N_HEADS), cache_k, cache_v, cache_logf, ppg=8)
        uext = jnp.concatenate([state_conv[l], u.reshape(bs, n_new, D_MODEL)], axis=1)
        zs = _conv_sample(uext, conv_w[l], cb, lng, lnb, n_new=n_new)
        xs = _merge(xs, mods_s[1][2], gpost[1], zs.reshape(ts, D_MODEL), attn.reshape(ts, D_MODEL), sga, sgc,
                    wpw, wout, tm=ts, tiles_per_block=1)
        xs = _ffn(xs, mods_s[2], gpre[2], gpost[2], *f2, tm=ts, tiles_per_block=1, res_w=0.5)
        outs[6].append(lf.reshape(bs, n_new, N_HEADS))
        outs[7].append(uext[:, n_new:, :])

    lf_p, conv_p, lf_s, conv_s = (jnp.stack(outs[i], axis=0) for i in (2, 3, 6, 7))
    k_p, v_p = (a.reshape(depth, bp, seq, N_HEADS, HEAD_DIM) for a in kv_p)
    k_s, v_s = (a.reshape(depth, bs, n_new, N_HEADS, HEAD_DIM) for a in kv_s)
    return (xp.reshape(bp, seq, D_MODEL), xs.reshape(bs, n_new, D_MODEL), k_p, v_p, lf_p, conv_p, k_s, v_s, lf_s, conv_s)
```

```python
import functools

import jax
import jax.numpy as jnp
from jax import lax
from jax.experimental import pallas as pl
from jax.experimental.pallas import tpu as pltpu

F32 = jnp.float32
BF16 = jnp.bfloat16
HIGHEST = lax.Precision.HIGHEST

D_MODEL = 1024
N_HEADS = 16
HEAD_DIM = 64
D_FF = 2816
CONV_W = 31
N_SUB = 3
EPS = 1e-6
NEG_INF = -1e30
QK_SCALE = HEAD_DIM ** -0.5
LOG2E = 1.4426950408889634
LANES = 128
HEADS_PER_LANE_TILE = LANES // HEAD_DIM
N_LANE_TILES = D_MODEL // LANES
CONV_HALO = 32

VMEM_LIMIT = 56 * 1024 * 1024


def _cparams(sem):
    return pltpu.CompilerParams(dimension_semantics=sem, vmem_limit_bytes=VMEM_LIMIT)


def _const_spec(shape):
    nd = len(shape)
    return pl.BlockSpec(shape, lambda *_: (0,) * nd, pipeline_mode=pl.Buffered(1))


def _rms(x, g):
    return x * lax.rsqrt(jnp.mean(x * x, axis=-1, keepdims=True) + EPS) * g


def _dot(a, b, **kw):
    return jnp.dot(a, b, preferred_element_type=F32, **kw)


def _dot_nt(a, b, **kw):
    return lax.dot_general(a, b, (((1,), (1,)), ((), ())), preferred_element_type=F32, **kw)


def _split3(x):
    hi = x.astype(BF16)
    r = x - hi.astype(F32)
    mid = r.astype(BF16)
    lo = (r - mid.astype(F32)).astype(BF16)
    return hi, mid, lo


def _mod_kernel(c_ref, w_ref, b_ref, o_ref):
    c = c_ref[...]
    a = (c * jax.nn.sigmoid(c)).astype(BF16)
    o_ref[0] = _dot(a, w_ref[0].astype(BF16)) + b_ref[0]


def _modulation(c_all, w_mod, b_mod):
    depth, _, n = w_mod.shape
    rows = c_all.shape[0]
    tn = 1152
    return pl.pallas_call(
        _mod_kernel,
        grid=(depth, n // tn),
        in_specs=[
            pl.BlockSpec((rows, D_MODEL), lambda l, j: (0, 0)),
            pl.BlockSpec((1, D_MODEL, tn), lambda l, j: (l, 0, j)),
            pl.BlockSpec((1, 1, tn), lambda l, j: (l, 0, j)),
        ],
        out_specs=pl.BlockSpec((1, rows, tn), lambda l, j: (l, 0, j)),
        out_shape=jax.ShapeDtypeStruct((depth, rows, n), F32),
        compiler_params=_cparams(("arbitrary", "arbitrary")),
        name="modulation",
    )(c_all, w_mod, b_mod.reshape(depth, 1, n))


def _mod_spec(m, tiles_per_block):
    rows = m.shape[1]
    return pl.BlockSpec((1, rows, D_MODEL), lambda i: (i // tiles_per_block, 0, 0))


def _ffn_kernel(x_ref, sh_ref, sc_ref, gt_ref, gpre_ref, gpost_ref, wg_ref, wu_ref, wd_ref, o_ref, *, res_w, ck):
    x = x_ref[...]
    h = (_rms(x, gpre_ref[...]) * (1.0 + sc_ref[0]) + sh_ref[0]).astype(BF16)
    acc = jnp.zeros(x.shape, F32)
    for c in range(D_FF // ck):
        g = _dot(h, wg_ref[:, c * ck:(c + 1) * ck])
        u = _dot(h, wu_ref[:, c * ck:(c + 1) * ck])
        a = (g * jax.nn.sigmoid(g) * u).astype(BF16)
        acc = acc + _dot(a, wd_ref[c * ck:(c + 1) * ck, :])
    o_ref[...] = x + res_w * gt_ref[0] * _rms(acc, gpost_ref[...])


def _ffn(x, mods, gpre, gpost, wg, wu, wd, *, tm, tiles_per_block, res_w):
    t = x.shape[0]
    sh, sc, gt = mods
    row = pl.BlockSpec((tm, D_MODEL), lambda i: (i, 0))
    return pl.pallas_call(
        functools.partial(_ffn_kernel, res_w=res_w, ck=D_FF),
        grid=(t // tm,),
        in_specs=[row, _mod_spec(sh, tiles_per_block), _mod_spec(sc, tiles_per_block),
                  _mod_spec(gt, tiles_per_block), _const_spec((1, D_MODEL)), _const_spec((1, D_MODEL)),
                  _const_spec(wg.shape), _const_spec(wu.shape), _const_spec(wd.shape)],
        out_specs=row,
        out_shape=jax.ShapeDtypeStruct((t, D_MODEL), F32),
        compiler_params=_cparams(("arbitrary",)),
        name="ffn",
    )(x, sh, sc, gt, gpre, gpost, wg, wu, wd)


def _proj_kernel(x_ref, sh_ref, sc_ref, gpre_ref, wqkv_ref, wf_ref, wr_ref, bf_ref, *rest,
                 tiles_per_seq, q_scale, n_aliased, transposed_kv):
    q_ref, k_ref, v_ref, kb_ref, vx_ref, lf_ref, fc_ref, u_ref, sga_ref, sgc_ref, carry_ref = rest[n_aliased:]
    x = x_ref[...]
    tm = x.shape[0]
    h = (_rms(x, gpre_ref[...]) * (1.0 + sc_ref[0]) + sh_ref[0]).astype(BF16)
    q_ref[...] = (_dot(h, wqkv_ref[:, 0:D_MODEL]) * q_scale).astype(BF16)
    k = _dot(h, wqkv_ref[:, D_MODEL:2 * D_MODEL])
    kb_ref[...] = k.astype(BF16)
    v = _dot(h, wqkv_ref[:, 2 * D_MODEL:3 * D_MODEL])
    if transposed_kv:
        vt = v.T
        k_ref[0, 0] = k.T
        v_ref[0, 0] = vt
        vx_ref[0] = vt.astype(BF16)
    else:
        k_ref[0] = k
        v_ref[0] = v
        vx_ref[...] = v.astype(BF16)

    f = _dot(h, wf_ref[...]) + bf_ref[...]
    lf = jnp.minimum(f, 0.0) - jnp.log1p(jnp.exp(-jnp.abs(f)))
    lf_ref[...] = lf

    @pl.when(pl.program_id(0) % tiles_per_seq == 0)
    def _():
        carry_ref[...] = jnp.zeros(carry_ref.shape, F32)

    r = lax.broadcasted_iota(jnp.int32, (tm, tm), 0)
    c = lax.broadcasted_iota(jnp.int32, (tm, tm), 1)
    tri = (c <= r).astype(F32)
    fc = _dot(tri, lf, precision=HIGHEST) + carry_ref[...]
    fc_ref[...] = fc
    carry_ref[...] = fc[tm - 1:tm, :]

    a = _dot(h, wr_ref[:, 0:D_MODEL])
    b = _dot(h, wr_ref[:, D_MODEL:2 * D_MODEL])
    u_ref[...] = a * jax.nn.sigmoid(b)
    sga_ref[...] = jax.nn.sigmoid(_dot(h, wr_ref[:, 2 * D_MODEL:3 * D_MODEL])).astype(BF16)
    sgc_ref[...] = jax.nn.sigmoid(_dot(h, wr_ref[:, 3 * D_MODEL:4 * D_MODEL])).astype(BF16)


def _proj(x, mods, gpre, wqkv, wf, wr, bf, *, tm, tiles_per_block, tiles_per_seq, q_scale, layer, depth, kv_stack,
          transposed_kv):
    t = x.shape[0]
    sh, sc, _ = mods
    row = pl.BlockSpec((tm, D_MODEL), lambda i: (i, 0))
    hrow = pl.BlockSpec((tm, N_HEADS), lambda i: (i, 0))
    wide = lambda dt: jax.ShapeDtypeStruct((t, D_MODEL), dt)
    narrow = jax.ShapeDtypeStruct((t, N_HEADS), F32)
    if transposed_kv:
        kv_spec = pl.BlockSpec((1, 1, D_MODEL, tm), lambda i: (layer, i // tiles_per_seq, 0, i % tiles_per_seq))
        kv_shape = jax.ShapeDtypeStruct((depth, t // (tm * tiles_per_seq), D_MODEL, tm * tiles_per_seq), F32)
        vx_spec = pl.BlockSpec((1, D_MODEL, tm), lambda i: (i, 0, 0))
        vx_shape = jax.ShapeDtypeStruct((t // tm, D_MODEL, tm), BF16)
    else:
        kv_spec = pl.BlockSpec((1, tm, D_MODEL), lambda i: (layer, i, 0))
        kv_shape = jax.ShapeDtypeStruct((depth, t, D_MODEL), F32)
        vx_spec, vx_shape = row, wide(BF16)
    n_fixed = 8
    return pl.pallas_call(
        functools.partial(_proj_kernel, tiles_per_seq=tiles_per_seq, q_scale=q_scale, n_aliased=len(kv_stack),
                          transposed_kv=transposed_kv),
        grid=(t // tm,),
        in_specs=[row, _mod_spec(sh, tiles_per_block), _mod_spec(sc, tiles_per_block),
                  _const_spec((1, D_MODEL)), _const_spec(wqkv.shape), _const_spec(wf.shape),
                  _const_spec(wr.shape), _const_spec((1, N_HEADS))]
        + [pl.BlockSpec(memory_space=pl.ANY) for _ in kv_stack],
        out_specs=[row, kv_spec, kv_spec, row, vx_spec, hrow, hrow, row, row, row],
        out_shape=[wide(BF16), kv_shape, kv_shape, wide(BF16), vx_shape, narrow, narrow,
                   wide(F32), wide(BF16), wide(BF16)],
        scratch_shapes=[pltpu.VMEM((1, N_HEADS), F32)],
        input_output_aliases={n_fixed + j: 1 + j for j in range(len(kv_stack))},
        compiler_params=_cparams(("arbitrary",)),
        name="proj",
    )(x, sh, sc, gpre, wqkv, wf, wr, bf, *kv_stack)


def _attn_prompt_kernel(q_ref, k_ref, vt_ref, frow_ref, fcol_ref, o_ref, qm_ref, m_ref, l_ref, acc_ref, *s_refs, tq):
    qi = pl.program_id(1)
    lane = lax.broadcasted_iota(jnp.int32, (tq, LANES), 1)
    first = lane < HEAD_DIM
    key_idx = lax.broadcasted_iota(jnp.int32, (tq, tq), 0)
    qry_idx = lax.broadcasted_iota(jnp.int32, (tq, tq), 1)
    causal = key_idx <= qry_idx

    for hp in range(N_LANE_TILES):
        qp = q_ref[0, :, hp * LANES:(hp + 1) * LANES]
        zero = jnp.zeros_like(qp)
        qm_ref[HEADS_PER_LANE_TILE * hp] = jnp.where(first, qp, zero)
        qm_ref[HEADS_PER_LANE_TILE * hp + 1] = jnp.where(first, zero, qp)
    m_ref[...] = jnp.full(m_ref.shape, NEG_INF, F32)
    l_ref[...] = jnp.zeros(l_ref.shape, F32)
    acc_ref[...] = jnp.zeros(acc_ref.shape, F32)

    def scores(j):
        start = pl.multiple_of(j * tq, tq)
        for h in range(N_HEADS):
            hp = h // HEADS_PER_LANE_TILE
            kb = k_ref[0, pl.ds(start, tq), hp * LANES:(hp + 1) * LANES]
            s_refs[h][...] = _dot_nt(kb, qm_ref[h])

    def softmax_pv(j, masked):
        start = pl.multiple_of(j * tq, tq)
        for h in range(N_HEADS):
            f_i = frow_ref[0, h:h + 1, :] * LOG2E
            f_j = fcol_ref[0, pl.ds(start, tq), h:h + 1] * LOG2E
            st = s_refs[h][...] + (f_i - f_j)
            if masked:
                st = jnp.where(causal, st, NEG_INF)
            m = m_ref[h]
            m_new = jnp.maximum(m, jnp.max(st, axis=0, keepdims=True))
            alpha = jnp.exp2(m - m_new)
            p = jnp.exp2(st - m_new)
            m_ref[h] = m_new
            l_ref[h] = alpha * l_ref[h] + jnp.sum(p, axis=0, keepdims=True)
            vt = vt_ref[0, j, h * HEAD_DIM:(h + 1) * HEAD_DIM, :]
            acc_ref[h] = alpha * acc_ref[h] + _dot(vt, p.astype(BF16))

    def body(j, c):
        softmax_pv(j - 1, False)
        scores(j)
        return c

    scores(0)
    lax.fori_loop(1, qi + 1, body, 0)
    softmax_pv(qi, True)
    for hp in range(N_LANE_TILES):
        h0 = HEADS_PER_LANE_TILE * hp
        ot = jnp.concatenate([acc_ref[h0] / l_ref[h0], acc_ref[h0 + 1] / l_ref[h0 + 1]], axis=0)
        o_ref[0, :, hp * LANES:(hp + 1) * LANES] = ot.T.astype(o_ref.dtype)


def _attn_prompt(q, kb, vt, frow, fcol, *, tq):
    b, t, _ = q.shape
    nk = t // tq
    return pl.pallas_call(
        functools.partial(_attn_prompt_kernel, tq=tq),
        grid=(b, nk),
        in_specs=[
            pl.BlockSpec((1, tq, D_MODEL), lambda bi, i: (bi, i, 0)),
            pl.BlockSpec((1, t, D_MODEL), lambda bi, i: (bi, 0, 0)),
            pl.BlockSpec((1, nk, D_MODEL, tq), lambda bi, i: (bi, 0, 0, 0)),
            pl.BlockSpec((1, N_HEADS, tq), lambda bi, i: (bi, 0, i)),
            pl.BlockSpec((1, t, N_HEADS), lambda bi, i: (bi, 0, 0)),
        ],
        out_specs=pl.BlockSpec((1, tq, D_MODEL), lambda bi, i: (bi, i, 0)),
        out_shape=jax.ShapeDtypeStruct((b, t, D_MODEL), BF16),
        scratch_shapes=[pltpu.VMEM((N_HEADS, tq, LANES), BF16), pltpu.VMEM((N_HEADS, 1, tq), F32),
                        pltpu.VMEM((N_HEADS, 1, tq), F32), pltpu.VMEM((N_HEADS, HEAD_DIM, tq), F32)]
        + [pltpu.VMEM((tq, tq), F32) for _ in range(N_HEADS)],
        compiler_params=_cparams(("arbitrary", "arbitrary")),
        name="attn_prompt",
    )(q, kb, vt, frow, fcol)


def _layer_norm_swish(z, g, b):
    mu = jnp.mean(z, axis=-1, keepdims=True)
    zc = z - mu
    var = jnp.mean(zc * zc, axis=-1, keepdims=True)
    y = zc * lax.rsqrt(var + EPS) * g + b
    return y * jax.nn.sigmoid(y)


def _conv_prompt_kernel(u_ref, cw_ref, cb_ref, lng_ref, lnb_ref, z_ref, uext_ref, zs_ref, *, tm, rchunk):
    ti = pl.program_id(1)
    for cbk in range(N_LANE_TILES):
        @pl.when(ti == 0)
        def _():
            uext_ref[cbk, 0:CONV_HALO, :] = jnp.zeros((CONV_HALO, LANES), F32)

        @pl.when(ti > 0)
        def _():
            uext_ref[cbk, 0:CONV_HALO, :] = uext_ref[cbk, tm:tm + CONV_HALO, :]

    for cbk in range(N_LANE_TILES):
        uext_ref[cbk, CONV_HALO:CONV_HALO + tm, :] = u_ref[0, :, cbk * LANES:(cbk + 1) * LANES]

    first_tap = CONV_HALO - (CONV_W - 1)

    def body(cbk, _):
        w = cw_ref[cbk]
        for rb in range(tm // rchunk):
            acc = jnp.zeros((rchunk, LANES), F32)
            for j in range(CONV_W):
                start = rb * rchunk + first_tap + j
                acc = acc + w[j:j + 1, :] * uext_ref[cbk, pl.ds(start, rchunk), :]
            zs_ref[cbk, rb * rchunk:(rb + 1) * rchunk, :] = acc
        return 0

    lax.fori_loop(0, N_LANE_TILES, body, 0)
    z = jnp.concatenate([zs_ref[cbk] for cbk in range(N_LANE_TILES)], axis=-1) + cb_ref[...]
    z_ref[0] = _layer_norm_swish(z, lng_ref[...], lnb_ref[...]).astype(z_ref.dtype)


def _conv_prompt(u, cw3, cb, lng, lnb, *, tm):
    b, t, _ = u.shape
    row = pl.BlockSpec((1, tm, D_MODEL), lambda bi, i: (bi, i, 0))
    return pl.pallas_call(
        functools.partial(_conv_prompt_kernel, tm=tm, rchunk=64),
        grid=(b, t // tm),
        in_specs=[row, pl.BlockSpec(cw3.shape, lambda bi, i: (0, 0, 0)),
                  pl.BlockSpec((1, D_MODEL), lambda bi, i: (0, 0)),
                  pl.BlockSpec((1, D_MODEL), lambda bi, i: (0, 0)),
                  pl.BlockSpec((1, D_MODEL), lambda bi, i: (0, 0))],
        out_specs=row,
        out_shape=jax.ShapeDtypeStruct((b, t, D_MODEL), BF16),
        scratch_shapes=[pltpu.VMEM((N_LANE_TILES, CONV_HALO + tm, LANES), F32),
                        pltpu.VMEM((N_LANE_TILES, tm, LANES), F32)],
        compiler_params=_cparams(("arbitrary", "arbitrary")),
        name="conv_prompt",
    )(u, cw3, cb, lng, lnb)


def _conv_sample_kernel(uext_ref, cw_ref, cb_ref, lng_ref, lnb_ref, z_ref, *, n_new):
    acc = jnp.zeros((n_new, D_MODEL), F32)
    for j in range(CONV_W):
        acc = acc + cw_ref[j:j + 1, :] * uext_ref[0, j:j + n_new, :]
    z_ref[0] = _layer_norm_swish(acc + cb_ref[...], lng_ref[...], lnb_ref[...])


def _conv_sample(uext, cw, cb, lng, lnb, *, n_new):
    b, rows, _ = uext.shape
    vec = pl.BlockSpec((1, D_MODEL), lambda i: (0, 0))
    return pl.pallas_call(
        functools.partial(_conv_sample_kernel, n_new=n_new),
        grid=(b,),
        in_specs=[pl.BlockSpec((1, rows, D_MODEL), lambda i: (i, 0, 0)),
                  pl.BlockSpec((CONV_W, D_MODEL), lambda i: (0, 0)), vec, vec, vec],
        out_specs=pl.BlockSpec((1, n_new, D_MODEL), lambda i: (i, 0, 0)),
        out_shape=jax.ShapeDtypeStruct((b, n_new, D_MODEL), F32),
        compiler_params=_cparams(("arbitrary",)),
        name="conv_sample",
    )(uext, cw, cb, lng, lnb)


def _merge_kernel(x_ref, gt_ref, gpost_ref, zs_ref, at_ref, sga_ref, sgc_ref, wpw_ref, wout_ref, o_ref):
    conv_out = _dot(zs_ref[...].astype(BF16), wpw_ref[...])
    mix = sga_ref[...].astype(F32) * at_ref[...].astype(F32) + sgc_ref[...].astype(F32) * conv_out
    y = _dot(mix.astype(BF16), wout_ref[...])
    o_ref[...] = x_ref[...] + gt_ref[0] * _rms(y, gpost_ref[...])


def _merge(x, gt, gpost, zs, attn, sga, sgc, wpw, wout, *, tm, tiles_per_block):
    t = x.shape[0]
    row = pl.BlockSpec((tm, D_MODEL), lambda i: (i, 0))
    return pl.pallas_call(
        _merge_kernel,
        grid=(t // tm,),
        in_specs=[row, _mod_spec(gt, tiles_per_block), _const_spec((1, D_MODEL)), row, row, row, row,
                  _const_spec(wpw.shape), _const_spec(wout.shape)],
        out_specs=row,
        out_shape=jax.ShapeDtypeStruct((t, D_MODEL), F32),
        compiler_params=_cparams(("arbitrary",)),
        name="merge",
    )(x, gt, gpost, zs, attn, sga, sgc, wpw, wout)


def _page_decays(lf_refs, carry, page):
    jr = lax.broadcasted_iota(jnp.int32, (page, page), 0)
    ic = lax.broadcasted_iota(jnp.int32, (page, page), 1)
    later = (jr > ic).astype(BF16)
    lfts = [ref[0, 0] for ref in lf_refs]
    parts = jnp.concatenate([part for lft in lfts for part in _split3(lft)], axis=0)
    d3 = _dot(parts, later)
    decays = []
    for r, lft in enumerate(lfts):
        lo = 3 * N_HEADS * r
        within = d3[lo:lo + N_HEADS] + d3[lo + N_HEADS:lo + 2 * N_HEADS] + d3[lo + 2 * N_HEADS:lo + 3 * N_HEADS]
        decays.append(within + carry)
        carry = carry + jnp.sum(lft, axis=-1, keepdims=True)
    return decays, carry


def _page_t(ref, page):
    return ref[0, 0].reshape(N_HEADS * HEAD_DIM, page).astype(BF16)


def _attn_sample_kernel(pt_ref, q_ref, kn_ref, vn_ref, lfn_ref, *rest, n_new, ppg, page):
    del pt_ref
    k_refs = rest[0:ppg]
    v_refs = rest[ppg:2 * ppg]
    lf_refs = rest[2 * ppg:3 * ppg]
    o_ref = rest[3 * ppg]
    qm_ref, m_ref, l_ref, acc_ref, fn_ref, carry_ref = rest[3 * ppg + 1:]
    g = pl.program_id(1)
    rows = n_new * N_HEADS
    kpad = kn_ref.shape[1]

    @pl.when(g == 0)
    def _():
        q4 = q_ref[0]
        qrep = jnp.concatenate([jnp.broadcast_to(q4[t:t + 1, :], (N_HEADS, D_MODEL)) for t in range(n_new)], axis=0)
        r = lax.broadcasted_iota(jnp.int32, (rows, D_MODEL), 0)
        c = lax.broadcasted_iota(jnp.int32, (rows, D_MODEL), 1)
        qm = jnp.where((c // HEAD_DIM) == (r % N_HEADS), qrep, 0.0).astype(BF16)
        qm_ref[...] = qm

        er = lax.broadcasted_iota(jnp.int32, (rows, N_HEADS), 0)
        ec = lax.broadcasted_iota(jnp.int32, (rows, N_HEADS), 1)
        expand = ((er % N_HEADS) == ec).astype(F32)
        lfn = lfn_ref[0]
        run = lfn[0:1, :]
        cums = [run]
        for t in range(1, n_new):
            run = run + lfn[t:t + 1, :]
            cums.append(run)
        fn = jnp.concatenate(cums + [jnp.zeros((kpad - n_new, N_HEADS), F32)], axis=0)
        fnt = _dot_nt(expand, fn, precision=HIGHEST)
        trow = lax.broadcasted_iota(jnp.int32, (rows, kpad), 0) // N_HEADS
        tcol = lax.broadcasted_iota(jnp.int32, (rows, kpad), 1)
        fcol = jnp.sum(jnp.where(tcol == trow, fnt, 0.0), axis=-1, keepdims=True)
        fn_ref[...] = fcol
        s = _dot_nt(qm, kn_ref[0]) + (fcol - fnt)
        s = jnp.where(tcol <= trow, s, NEG_INF)
        m = jnp.max(s, axis=-1, keepdims=True)
        p = jnp.exp(s - m)
        m_ref[...] = m
        l_ref[...] = jnp.sum(p, axis=-1, keepdims=True)
        acc_ref[...] = _dot(p.astype(BF16), vn_ref[0])
        carry_ref[...] = jnp.zeros(carry_ref.shape, F32)

    qm = qm_ref[...]
    fcol = fn_ref[...]
    kt = jnp.concatenate([_page_t(k_refs[r], page) for r in range(ppg)], axis=1)
    vt = jnp.concatenate([_page_t(v_refs[r], page) for r in range(ppg)], axis=1)
    decays, carry = _page_decays(lf_refs, carry_ref[...], page)
    carry_ref[...] = carry
    decay = jnp.concatenate([jnp.concatenate([d] * n_new, axis=0) for d in decays], axis=1)
    s = _dot(qm, kt) + (fcol + decay)
    m = m_ref[...]
    m_new = jnp.maximum(m, jnp.max(s, axis=-1, keepdims=True))
    alpha = jnp.exp(m - m_new)
    p = jnp.exp(s - m_new)
    m_ref[...] = m_new
    l_ref[...] = alpha * l_ref[...] + jnp.sum(p, axis=-1, keepdims=True)
    acc_ref[...] = alpha * acc_ref[...] + _dot_nt(p.astype(BF16), vt)

    @pl.when(g == pl.num_programs(1) - 1)
    def _():
        o = acc_ref[...] / l_ref[...]
        r = lax.broadcasted_iota(jnp.int32, (rows, D_MODEL), 0)
        c = lax.broadcasted_iota(jnp.int32, (rows, D_MODEL), 1)
        o = jnp.where((c // HEAD_DIM) == (r % N_HEADS), o, 0.0)
        o_ref[0] = jnp.concatenate(
            [jnp.sum(o[t * N_HEADS:(t + 1) * N_HEADS, :], axis=0, keepdims=True) for t in range(n_new)], axis=0)


def _attn_sample(layer, page_table, q, kn, vn, lfn, cache_kt, cache_vt, cache_lft, *, ppg):
    b, n_new, _ = q.shape
    n_pages = page_table.shape[1]
    page = cache_kt.shape[-1]
    kpad = kn.shape[1]
    rows = n_new * N_HEADS
    groups = n_pages // ppg

    def page_spec(r, lead):
        def imap(s, g, pt):
            return (layer, pt[s * n_pages + (n_pages - 1 - (g * ppg + r))]) + (0,) * (len(lead) + 1)
        return pl.BlockSpec((1, 1) + lead + (page,), imap)

    per_seq = lambda nrow, width: pl.BlockSpec((1, nrow, width), lambda s, g, pt: (s, 0, 0))
    in_specs = ([per_seq(n_new, D_MODEL), per_seq(kpad, D_MODEL), per_seq(kpad, D_MODEL), per_seq(n_new, N_HEADS)]
                + [page_spec(r, (N_HEADS, HEAD_DIM)) for r in range(ppg)]
                + [page_spec(r, (N_HEADS, HEAD_DIM)) for r in range(ppg)]
                + [page_spec(r, (N_HEADS,)) for r in range(ppg)])
    grid_spec = pltpu.PrefetchScalarGridSpec(
        num_scalar_prefetch=1,
        grid=(b, groups),
        in_specs=in_specs,
        out_specs=per_seq(n_new, D_MODEL),
        scratch_shapes=[pltpu.VMEM((rows, D_MODEL), BF16), pltpu.VMEM((rows, 1), F32), pltpu.VMEM((rows, 1), F32),
                        pltpu.VMEM((rows, D_MODEL), F32), pltpu.VMEM((rows, 1), F32), pltpu.VMEM((N_HEADS, 1), F32)],
    )
    return pl.pallas_call(
        functools.partial(_attn_sample_kernel, n_new=n_new, ppg=ppg, page=page),
        grid_spec=grid_spec,
        out_shape=jax.ShapeDtypeStruct((b, n_new, D_MODEL), F32),
        compiler_params=_cparams(("arbitrary", "arbitrary")),
        name="attn_sample",
    )(page_table.reshape(-1), q, kn, vn, lfn, *([cache_kt] * ppg), *([cache_vt] * ppg), *([cache_lft] * ppg))


def kernel(x_prompt, x_sample, c_prompt, c_sample, cache_k, cache_v, cache_logf, state_conv, page_table,
           norm_pre, norm_post, w_mod, b_mod, ffn1_gate, ffn1_up, ffn1_down, w_in, b_forget,
           conv_w, conv_b, conv_ln_g, conv_ln_b, w_pw, w_out, ffn2_gate, ffn2_up, ffn2_down):
    depth = w_mod.shape[0]
    bp, seq, _ = x_prompt.shape
    bs, n_new, _ = x_sample.shape
    n_phys, page = cache_k.shape[1], cache_k.shape[2]
    tp, ts = bp * seq, bs * n_new
    d_attn = N_HEADS * HEAD_DIM
    assert d_attn == D_MODEL and w_in.shape[2] == 3 * d_attn + N_HEADS + 4 * D_MODEL

    tm_ffn, tm_proj, tq, tm_conv, tm_merge = 512, 256, 256, 256, 512
    assert tm_proj == tq
    kpad = 16

    mod = _modulation(jnp.concatenate([c_prompt, c_sample], axis=0), w_mod, b_mod)
    mod = mod.reshape(depth, bp + bs, N_SUB, 3, D_MODEL)

    xp = x_prompt.reshape(tp, D_MODEL)
    xs = x_sample.reshape(ts, D_MODEL)
    outs = [[] for _ in range(8)]
    kv_p, kv_s = (), ()
    cache_kt = cache_k.transpose(0, 1, 3, 4, 2)
    cache_vt = cache_v.transpose(0, 1, 3, 4, 2)
    cache_lft = cache_logf.transpose(0, 1, 3, 2)
    for l in range(depth):
        wqkv = w_in[l, :, :3 * d_attn].astype(BF16)
        wf = w_in[l, :, 3 * d_attn:3 * d_attn + N_HEADS].astype(BF16)
        wr = w_in[l, :, 3 * d_attn + N_HEADS:].astype(BF16)
        bf = b_forget[l].reshape(1, N_HEADS)
        f1 = (ffn1_gate[l].astype(BF16), ffn1_up[l].astype(BF16), ffn1_down[l].astype(BF16))
        f2 = (ffn2_gate[l].astype(BF16), ffn2_up[l].astype(BF16), ffn2_down[l].astype(BF16))
        wpw = w_pw[l].astype(BF16)
        wout = w_out[l].astype(BF16)
        gpre = [norm_pre[l, s].reshape(1, D_MODEL) for s in range(N_SUB)]
        gpost = [norm_post[l, s].reshape(1, D_MODEL) for s in range(N_SUB)]
        cb = conv_b[l].reshape(1, D_MODEL)
        lng = conv_ln_g[l].reshape(1, D_MODEL)
        lnb = conv_ln_b[l].reshape(1, D_MODEL)
        cw3 = conv_w[l].reshape(CONV_W, N_LANE_TILES, LANES).transpose(1, 0, 2)

        mods_p = [[mod[l, :bp, s, j].reshape(bp, 1, D_MODEL) for j in range(3)] for s in range(N_SUB)]
        mods_s = [[jnp.repeat(mod[l, bp:, s, j], n_new, axis=0).reshape(1, ts, D_MODEL) for j in range(3)]
                  for s in range(N_SUB)]

        xp = _ffn(xp, mods_p[0], gpre[0], gpost[0], *f1, tm=tm_ffn, tiles_per_block=seq // tm_ffn, res_w=0.5)
        q, k, v, kb, vt, lf, fc, u, sga, sgc = _proj(
            xp, mods_p[1], gpre[1], wqkv, wf, wr, bf, tm=tm_proj, tiles_per_block=seq // tm_proj,
            tiles_per_seq=seq // tm_proj, q_scale=QK_SCALE * LOG2E, layer=l, depth=depth, kv_stack=kv_p,
            transposed_kv=True)
        kv_p = (k, v)
        fcol = fc.reshape(bp, seq, N_HEADS)
        attn = _attn_prompt(q.reshape(bp, seq, D_MODEL), kb.reshape(bp, seq, D_MODEL),
                            vt.reshape(bp, seq // tq, D_MODEL, tq), fcol.transpose(0, 2, 1), fcol, tq=tq)
        u3 = u.reshape(bp, seq, D_MODEL)
        zs = _conv_prompt(u3, cw3, cb, lng, lnb, tm=tm_conv)
        xp = _merge(xp, mods_p[1][2], gpost[1], zs.reshape(tp, D_MODEL), attn.reshape(tp, D_MODEL), sga, sgc,
                    wpw, wout, tm=tm_merge, tiles_per_block=seq // tm_merge)
        xp = _ffn(xp, mods_p[2], gpre[2], gpost[2], *f2, tm=tm_ffn, tiles_per_block=seq // tm_ffn, res_w=0.5)
        outs[2].append(lf.reshape(bp, seq, N_HEADS))
        outs[3].append(u3[:, seq - (CONV_W - 1):, :])

        xs = _ffn(xs, mods_s[0], gpre[0], gpost[0], *f1, tm=ts, tiles_per_block=1, res_w=0.5)
        q, k, v, kb, vb, lf, _, u, sga, sgc = _proj(
            xs, mods_s[1], gpre[1], wqkv, wf, wr, bf, tm=ts, tiles_per_block=1, tiles_per_seq=1, q_scale=QK_SCALE,
            layer=l, depth=depth, kv_stack=kv_s, transposed_kv=False)
        kv_s = (k, v)
        pad = lambda a: jnp.pad(a.reshape(bs, n_new, D_MODEL), ((0, 0), (0, kpad - n_new), (0, 0)))
        attn = _attn_sample(l, page_table, (q.astype(F32)).reshape(bs, n_new, D_MODEL), pad(kb), pad(vb),
                            lf.reshape(bs, n_new, N_HEADS), cache_kt, cache_vt, cache_lft, ppg=8)
        uext = jnp.concatenate([state_conv[l], u.reshape(bs, n_new, D_MODEL)], axis=1)
        zs = _conv_sample(uext, conv_w[l], cb, lng, lnb, n_new=n_new)
        xs = _merge(xs, mods_s[1][2], gpost[1], zs.reshape(ts, D_MODEL), attn.reshape(ts, D_MODEL), sga, sgc,
                    wpw, wout, tm=ts, tiles_per_block=1)
        xs = _ffn(xs, mods_s[2], gpre[2], gpost[2], *f2, tm=ts, tiles_per_block=1, res_w=0.5)
        outs[6].append(lf.reshape(bs, n_new, N_HEADS))
        outs[7].append(uext[:, n_new:, :])

    lf_p, conv_p, lf_s, conv_s = (jnp.stack(outs[i], axis=0) for i in (2, 3, 6, 7))
    k_p, v_p = (a.reshape(depth, bp, N_HEADS, HEAD_DIM, seq).transpose(0, 1, 4, 2, 3) for a in kv_p)
    k_s, v_s = (a.reshape(depth, bs, n_new, N_HEADS, HEAD_DIM) for a in kv_s)
    return (xp.reshape(bp, seq, D_MODEL), xs.reshape(bs, n_new, D_MODEL), k_p, v_p, lf_p, conv_p, k_s, v_s, lf_s, conv_s)
```

```python
import functools

import jax
import jax.numpy as jnp
from jax import lax
from jax.experimental import pallas as pl
from jax.experimental.pallas import tpu as pltpu

F32 = jnp.float32
BF16 = jnp.bfloat16
HIGHEST = lax.Precision.HIGHEST

D_MODEL = 1024
N_HEADS = 16
HEAD_DIM = 64
D_FF = 2816
CONV_W = 31
N_SUB = 3
EPS = 1e-6
NEG_INF = -1e30
QK_SCALE = HEAD_DIM ** -0.5
LOG2E = 1.4426950408889634
LANES = 128
HEADS_PER_LANE_TILE = LANES // HEAD_DIM
N_LANE_TILES = D_MODEL // LANES
CONV_HALO = 32

VMEM_LIMIT = 56 * 1024 * 1024


def _cparams(sem):
    return pltpu.CompilerParams(dimension_semantics=sem, vmem_limit_bytes=VMEM_LIMIT)


def _const_spec(shape):
    nd = len(shape)
    return pl.BlockSpec(shape, lambda *_: (0,) * nd, pipeline_mode=pl.Buffered(1))


def _rms(x, g):
    return x * lax.rsqrt(jnp.mean(x * x, axis=-1, keepdims=True) + EPS) * g


def _dot(a, b, **kw):
    return jnp.dot(a, b, preferred_element_type=F32, **kw)


def _dot_nt(a, b, **kw):
    return lax.dot_general(a, b, (((1,), (1,)), ((), ())), preferred_element_type=F32, **kw)


def _split3(x):
    hi = x.astype(BF16)
    r = x - hi.astype(F32)
    mid = r.astype(BF16)
    lo = (r - mid.astype(F32)).astype(BF16)
    return hi, mid, lo


def _mod_kernel(c_ref, w_ref, b_ref, o_ref):
    c = c_ref[...]
    a = (c * jax.nn.sigmoid(c)).astype(BF16)
    o_ref[0] = _dot(a, w_ref[0].astype(BF16)) + b_ref[0]


def _modulation(c_all, w_mod, b_mod):
    depth, _, n = w_mod.shape
    rows = c_all.shape[0]
    tn = 1152
    return pl.pallas_call(
        _mod_kernel,
        grid=(depth, n // tn),
        in_specs=[
            pl.BlockSpec((rows, D_MODEL), lambda l, j: (0, 0)),
            pl.BlockSpec((1, D_MODEL, tn), lambda l, j: (l, 0, j)),
            pl.BlockSpec((1, 1, tn), lambda l, j: (l, 0, j)),
        ],
        out_specs=pl.BlockSpec((1, rows, tn), lambda l, j: (l, 0, j)),
        out_shape=jax.ShapeDtypeStruct((depth, rows, n), F32),
        compiler_params=_cparams(("arbitrary", "arbitrary")),
        name="modulation",
    )(c_all, w_mod, b_mod.reshape(depth, 1, n))


def _mod_spec(m, tiles_per_block):
    rows = m.shape[1]
    return pl.BlockSpec((1, rows, D_MODEL), lambda i: (i // tiles_per_block, 0, 0))


def _ffn_kernel(x_ref, sh_ref, sc_ref, gt_ref, gpre_ref, gpost_ref, wg_ref, wu_ref, wd_ref, o_ref, *, res_w, ck):
    x = x_ref[...]
    h = (_rms(x, gpre_ref[...]) * (1.0 + sc_ref[0]) + sh_ref[0]).astype(BF16)
    acc = jnp.zeros(x.shape, F32)
    for c in range(D_FF // ck):
        g = _dot(h, wg_ref[:, c * ck:(c + 1) * ck])
        u = _dot(h, wu_ref[:, c * ck:(c + 1) * ck])
        a = (g * jax.nn.sigmoid(g) * u).astype(BF16)
        acc = acc + _dot(a, wd_ref[c * ck:(c + 1) * ck, :])
    o_ref[...] = x + res_w * gt_ref[0] * _rms(acc, gpost_ref[...])


def _ffn(x, mods, gpre, gpost, wg, wu, wd, *, tm, tiles_per_block, res_w):
    t = x.shape[0]
    sh, sc, gt = mods
    row = pl.BlockSpec((tm, D_MODEL), lambda i: (i, 0))
    return pl.pallas_call(
        functools.partial(_ffn_kernel, res_w=res_w, ck=D_FF),
        grid=(t // tm,),
        in_specs=[row, _mod_spec(sh, tiles_per_block), _mod_spec(sc, tiles_per_block),
                  _mod_spec(gt, tiles_per_block), _const_spec((1, D_MODEL)), _const_spec((1, D_MODEL)),
                  _const_spec(wg.shape), _const_spec(wu.shape), _const_spec(wd.shape)],
        out_specs=row,
        out_shape=jax.ShapeDtypeStruct((t, D_MODEL), F32),
        compiler_params=_cparams(("arbitrary",)),
        name="ffn",
    )(x, sh, sc, gt, gpre, gpost, wg, wu, wd)


def _proj_kernel(x_ref, sh_ref, sc_ref, gpre_ref, wqkv_ref, wf_ref, wr_ref, bf_ref, *rest,
                 tiles_per_seq, q_scale, n_aliased, transposed_kv):
    q_ref, k_ref, v_ref, kb_ref, vx_ref, lf_ref, fc_ref, u_ref, sga_ref, sgc_ref, carry_ref = rest[n_aliased:]
    x = x_ref[...]
    tm = x.shape[0]
    h = (_rms(x, gpre_ref[...]) * (1.0 + sc_ref[0]) + sh_ref[0]).astype(BF16)
    q_ref[...] = (_dot(h, wqkv_ref[:, 0:D_MODEL]) * q_scale).astype(BF16)
    k = _dot(h, wqkv_ref[:, D_MODEL:2 * D_MODEL])
    kb_ref[...] = k.astype(BF16)
    v = _dot(h, wqkv_ref[:, 2 * D_MODEL:3 * D_MODEL])
    if transposed_kv:
        vt = v.T
        k_ref[0, 0] = k.T
        v_ref[0, 0] = vt
        vx_ref[0] = vt.astype(BF16)
    else:
        k_ref[0] = k
        v_ref[0] = v
        vx_ref[...] = v.astype(BF16)

    f = _dot(h, wf_ref[...]) + bf_ref[...]
    lf = jnp.minimum(f, 0.0) - jnp.log1p(jnp.exp(-jnp.abs(f)))
    lf_ref[...] = lf

    @pl.when(pl.program_id(0) % tiles_per_seq == 0)
    def _():
        carry_ref[...] = jnp.zeros(carry_ref.shape, F32)

    r = lax.broadcasted_iota(jnp.int32, (tm, tm), 0)
    c = lax.broadcasted_iota(jnp.int32, (tm, tm), 1)
    tri = (c <= r).astype(BF16)
    d3 = _dot(tri, jnp.concatenate(_split3(lf), axis=1))
    fc = (d3[:, 0:N_HEADS] + d3[:, N_HEADS:2 * N_HEADS] + d3[:, 2 * N_HEADS:3 * N_HEADS]) + carry_ref[...]
    fc_ref[...] = fc
    carry_ref[...] = fc[tm - 1:tm, :]

    a = _dot(h, wr_ref[:, 0:D_MODEL])
    b = _dot(h, wr_ref[:, D_MODEL:2 * D_MODEL])
    u_ref[...] = a * jax.nn.sigmoid(b)
    sga_ref[...] = jax.nn.sigmoid(_dot(h, wr_ref[:, 2 * D_MODEL:3 * D_MODEL])).astype(BF16)
    sgc_ref[...] = jax.nn.sigmoid(_dot(h, wr_ref[:, 3 * D_MODEL:4 * D_MODEL])).astype(BF16)


def _proj(x, mods, gpre, wqkv, wf, wr, bf, *, tm, tiles_per_block, tiles_per_seq, q_scale, layer, depth, kv_stack,
          transposed_kv):
    t = x.shape[0]
    sh, sc, _ = mods
    row = pl.BlockSpec((tm, D_MODEL), lambda i: (i, 0))
    hrow = pl.BlockSpec((tm, N_HEADS), lambda i: (i, 0))
    wide = lambda dt: jax.ShapeDtypeStruct((t, D_MODEL), dt)
    narrow = jax.ShapeDtypeStruct((t, N_HEADS), F32)
    if transposed_kv:
        kv_spec = pl.BlockSpec((1, 1, D_MODEL, tm), lambda i: (layer, i // tiles_per_seq, 0, i % tiles_per_seq))
        kv_shape = jax.ShapeDtypeStruct((depth, t // (tm * tiles_per_seq), D_MODEL, tm * tiles_per_seq), F32)
        vx_spec = pl.BlockSpec((1, D_MODEL, tm), lambda i: (i, 0, 0))
        vx_shape = jax.ShapeDtypeStruct((t // tm, D_MODEL, tm), BF16)
    else:
        kv_spec = pl.BlockSpec((1, tm, D_MODEL), lambda i: (layer, i, 0))
        kv_shape = jax.ShapeDtypeStruct((depth, t, D_MODEL), F32)
        vx_spec, vx_shape = row, wide(BF16)
    n_fixed = 8
    return pl.pallas_call(
        functools.partial(_proj_kernel, tiles_per_seq=tiles_per_seq, q_scale=q_scale, n_aliased=len(kv_stack),
                          transposed_kv=transposed_kv),
        grid=(t // tm,),
        in_specs=[row, _mod_spec(sh, tiles_per_block), _mod_spec(sc, tiles_per_block),
                  _const_spec((1, D_MODEL)), _const_spec(wqkv.shape), _const_spec(wf.shape),
                  _const_spec(wr.shape), _const_spec((1, N_HEADS))]
        + [pl.BlockSpec(memory_space=pl.ANY) for _ in kv_stack],
        out_specs=[row, kv_spec, kv_spec, row, vx_spec, hrow, hrow, row, row, row],
        out_shape=[wide(BF16), kv_shape, kv_shape, wide(BF16), vx_shape, narrow, narrow,
                   wide(F32), wide(BF16), wide(BF16)],
        scratch_shapes=[pltpu.VMEM((1, N_HEADS), F32)],
        input_output_aliases={n_fixed + j: 1 + j for j in range(len(kv_stack))},
        compiler_params=_cparams(("arbitrary",)),
        name="proj",
    )(x, sh, sc, gpre, wqkv, wf, wr, bf, *kv_stack)


def _attn_prompt_kernel(q_ref, k_ref, vt_ref, frow_ref, fcol_ref, o_ref, qm_ref, m_ref, l_ref, acc_ref, *s_refs, tq):
    qi = pl.program_id(1)
    lane = lax.broadcasted_iota(jnp.int32, (tq, LANES), 1)
    first = lane < HEAD_DIM
    key_idx = lax.broadcasted_iota(jnp.int32, (tq, tq), 0)
    qry_idx = lax.broadcasted_iota(jnp.int32, (tq, tq), 1)
    causal = key_idx <= qry_idx

    for hp in range(N_LANE_TILES):
        qp = q_ref[0, :, hp * LANES:(hp + 1) * LANES]
        zero = jnp.zeros_like(qp)
        qm_ref[HEADS_PER_LANE_TILE * hp] = jnp.where(first, qp, zero)
        qm_ref[HEADS_PER_LANE_TILE * hp + 1] = jnp.where(first, zero, qp)
    m_ref[...] = jnp.full(m_ref.shape, NEG_INF, F32)
    l_ref[...] = jnp.zeros(l_ref.shape, F32)
    acc_ref[...] = jnp.zeros(acc_ref.shape, F32)

    def scores(j):
        start = pl.multiple_of(j * tq, tq)
        for h in range(N_HEADS):
            hp = h // HEADS_PER_LANE_TILE
            kb = k_ref[0, pl.ds(start, tq), hp * LANES:(hp + 1) * LANES]
            s_refs[h][...] = _dot_nt(kb, qm_ref[h])

    def softmax_pv(j, masked):
        start = pl.multiple_of(j * tq, tq)
        for h in range(N_HEADS):
            f_i = frow_ref[0, h:h + 1, :] * LOG2E
            f_j = fcol_ref[0, pl.ds(start, tq), h:h + 1] * LOG2E
            st = s_refs[h][...] - f_j
            if masked:
                st = jnp.where(causal, st, NEG_INF)
            m = m_ref[h]
            m_new = jnp.maximum(m, jnp.max(st, axis=0, keepdims=True) + f_i)
            alpha = jnp.exp2(m - m_new)
            p = jnp.exp2(st - (m_new - f_i))
            m_ref[h] = m_new
            l_ref[h] = alpha * l_ref[h] + jnp.sum(p, axis=0, keepdims=True)
            vt = vt_ref[0, j, h * HEAD_DIM:(h + 1) * HEAD_DIM, :]
            acc_ref[h] = alpha * acc_ref[h] + _dot(vt, p.astype(BF16))

    def body(j, c):
        softmax_pv(j - 1, False)
        scores(j)
        return c

    scores(0)
    lax.fori_loop(1, qi + 1, body, 0)
    softmax_pv(qi, True)
    for hp in range(N_LANE_TILES):
        h0 = HEADS_PER_LANE_TILE * hp
        ot = jnp.concatenate([acc_ref[h0] / l_ref[h0], acc_ref[h0 + 1] / l_ref[h0 + 1]], axis=0)
        o_ref[0, :, hp * LANES:(hp + 1) * LANES] = ot.T.astype(o_ref.dtype)


def _attn_prompt(q, kb, vt, frow, fcol, *, tq):
    b, t, _ = q.shape
    nk = t // tq
    return pl.pallas_call(
        functools.partial(_attn_prompt_kernel, tq=tq),
        grid=(b, nk),
        in_specs=[
            pl.BlockSpec((1, tq, D_MODEL), lambda bi, i: (bi, i, 0)),
            pl.BlockSpec((1, t, D_MODEL), lambda bi, i: (bi, 0, 0)),
            pl.BlockSpec((1, nk, D_MODEL, tq), lambda bi, i: (bi, 0, 0, 0)),
            pl.BlockSpec((1, N_HEADS, tq), lambda bi, i: (bi, 0, i)),
            pl.BlockSpec((1, t, N_HEADS), lambda bi, i: (bi, 0, 0)),
        ],
        out_specs=pl.BlockSpec((1, tq, D_MODEL), lambda bi, i: (bi, i, 0)),
        out_shape=jax.ShapeDtypeStruct((b, t, D_MODEL), BF16),
        scratch_shapes=[pltpu.VMEM((N_HEADS, tq, LANES), BF16), pltpu.VMEM((N_HEADS, 1, tq), F32),
                        pltpu.VMEM((N_HEADS, 1, tq), F32), pltpu.VMEM((N_HEADS, HEAD_DIM, tq), F32)]
        + [pltpu.VMEM((tq, tq), F32) for _ in range(N_HEADS)],
        compiler_params=_cparams(("arbitrary", "arbitrary")),
        name="attn_prompt",
    )(q, kb, vt, frow, fcol)


def _layer_norm_swish(z, g, b):
    mu = jnp.mean(z, axis=-1, keepdims=True)
    zc = z - mu
    var = jnp.mean(zc * zc, axis=-1, keepdims=True)
    y = zc * lax.rsqrt(var + EPS) * g + b
    return y * jax.nn.sigmoid(y)


def _conv_prompt_kernel(u_ref, cw_ref, cb_ref, lng_ref, lnb_ref, z_ref, uext_ref, zs_ref, *, tm, rchunk):
    ti = pl.program_id(1)
    for cbk in range(N_LANE_TILES):
        @pl.when(ti == 0)
        def _():
            uext_ref[cbk, 0:CONV_HALO, :] = jnp.zeros((CONV_HALO, LANES), F32)

        @pl.when(ti > 0)
        def _():
            uext_ref[cbk, 0:CONV_HALO, :] = uext_ref[cbk, tm:tm + CONV_HALO, :]

    for cbk in range(N_LANE_TILES):
        uext_ref[cbk, CONV_HALO:CONV_HALO + tm, :] = u_ref[0, :, cbk * LANES:(cbk + 1) * LANES]

    first_tap = CONV_HALO - (CONV_W - 1)

    def body(cbk, _):
        w = cw_ref[cbk]
        for rb in range(tm // rchunk):
            acc = jnp.zeros((rchunk, LANES), F32)
            for j in range(CONV_W):
                start = rb * rchunk + first_tap + j
                acc = acc + w[j:j + 1, :] * uext_ref[cbk, pl.ds(start, rchunk), :]
            zs_ref[cbk, rb * rchunk:(rb + 1) * rchunk, :] = acc
        return 0

    lax.fori_loop(0, N_LANE_TILES, body, 0)
    z = jnp.concatenate([zs_ref[cbk] for cbk in range(N_LANE_TILES)], axis=-1) + cb_ref[...]
    z_ref[0] = _layer_norm_swish(z, lng_ref[...], lnb_ref[...]).astype(z_ref.dtype)


def _conv_prompt(u, cw3, cb, lng, lnb, *, tm):
    b, t, _ = u.shape
    row = pl.BlockSpec((1, tm, D_MODEL), lambda bi, i: (bi, i, 0))
    return pl.pallas_call(
        functools.partial(_conv_prompt_kernel, tm=tm, rchunk=64),
        grid=(b, t // tm),
        in_specs=[row, pl.BlockSpec(cw3.shape, lambda bi, i: (0, 0, 0)),
                  pl.BlockSpec((1, D_MODEL), lambda bi, i: (0, 0)),
                  pl.BlockSpec((1, D_MODEL), lambda bi, i: (0, 0)),
                  pl.BlockSpec((1, D_MODEL), lambda bi, i: (0, 0))],
        out_specs=row,
        out_shape=jax.ShapeDtypeStruct((b, t, D_MODEL), BF16),
        scratch_shapes=[pltpu.VMEM((N_LANE_TILES, CONV_HALO + tm, LANES), F32),
                        pltpu.VMEM((N_LANE_TILES, tm, LANES), F32)],
        compiler_params=_cparams(("arbitrary", "arbitrary")),
        name="conv_prompt",
    )(u, cw3, cb, lng, lnb)


def _conv_sample_kernel(uext_ref, cw_ref, cb_ref, lng_ref, lnb_ref, z_ref, *, n_new):
    acc = jnp.zeros((n_new, D_MODEL), F32)
    for j in range(CONV_W):
        acc = acc + cw_ref[j:j + 1, :] * uext_ref[0, j:j + n_new, :]
    z_ref[0] = _layer_norm_swish(acc + cb_ref[...], lng_ref[...], lnb_ref[...])


def _conv_sample(uext, cw, cb, lng, lnb, *, n_new):
    b, rows, _ = uext.shape
    vec = pl.BlockSpec((1, D_MODEL), lambda i: (0, 0))
    return pl.pallas_call(
        functools.partial(_conv_sample_kernel, n_new=n_new),
        grid=(b,),
        in_specs=[pl.BlockSpec((1, rows, D_MODEL), lambda i: (i, 0, 0)),
                  pl.BlockSpec((CONV_W, D_MODEL), lambda i: (0, 0)), vec, vec, vec],
        out_specs=pl.BlockSpec((1, n_new, D_MODEL), lambda i: (i, 0, 0)),
        out_shape=jax.ShapeDtypeStruct((b, n_new, D_MODEL), F32),
        compiler_params=_cparams(("arbitrary",)),
        name="conv_sample",
    )(uext, cw, cb, lng, lnb)


def _merge_kernel(x_ref, gt_ref, gpost_ref, zs_ref, at_ref, sga_ref, sgc_ref, wpw_ref, wout_ref, o_ref):
    conv_out = _dot(zs_ref[...].astype(BF16), wpw_ref[...])
    mix = sga_ref[...].astype(F32) * at_ref[...].astype(F32) + sgc_ref[...].astype(F32) * conv_out
    y = _dot(mix.astype(BF16), wout_ref[...])
    o_ref[...] = x_ref[...] + gt_ref[0] * _rms(y, gpost_ref[...])


def _merge(x, gt, gpost, zs, attn, sga, sgc, wpw, wout, *, tm, tiles_per_block):
    t = x.shape[0]
    row = pl.BlockSpec((tm, D_MODEL), lambda i: (i, 0))
    return pl.pallas_call(
        _merge_kernel,
        grid=(t // tm,),
        in_specs=[row, _mod_spec(gt, tiles_per_block), _const_spec((1, D_MODEL)), row, row, row, row,
                  _const_spec(wpw.shape), _const_spec(wout.shape)],
        out_specs=row,
        out_shape=jax.ShapeDtypeStruct((t, D_MODEL), F32),
        compiler_params=_cparams(("arbitrary",)),
        name="merge",
    )(x, gt, gpost, zs, attn, sga, sgc, wpw, wout)


def _page_decays(lf_refs, carry, page):
    jr = lax.broadcasted_iota(jnp.int32, (page, page), 0)
    ic = lax.broadcasted_iota(jnp.int32, (page, page), 1)
    later = (jr > ic).astype(BF16)
    lfts = [ref[0, 0] for ref in lf_refs]
    parts = jnp.concatenate([part for lft in lfts for part in _split3(lft)], axis=0)
    d3 = _dot(parts, later)
    decays = []
    for r, lft in enumerate(lfts):
        lo = 3 * N_HEADS * r
        within = d3[lo:lo + N_HEADS] + d3[lo + N_HEADS:lo + 2 * N_HEADS] + d3[lo + 2 * N_HEADS:lo + 3 * N_HEADS]
        decays.append(within + carry)
        carry = carry + jnp.sum(lft, axis=-1, keepdims=True)
    return decays, carry


def _page_t(ref, page):
    return ref[0, 0].reshape(N_HEADS * HEAD_DIM, page).astype(BF16)


def _attn_sample_kernel(pt_ref, q_ref, kn_ref, vn_ref, lfn_ref, *rest, n_new, ppg, page):
    del pt_ref
    k_refs = rest[0:ppg]
    v_refs = rest[ppg:2 * ppg]
    lf_refs = rest[2 * ppg:3 * ppg]
    o_ref = rest[3 * ppg]
    qm_ref, m_ref, l_ref, acc_ref, fn_ref, carry_ref = rest[3 * ppg + 1:]
    g = pl.program_id(1)
    rows = n_new * N_HEADS
    kpad = kn_ref.shape[1]

    @pl.when(g == 0)
    def _():
        q4 = q_ref[0]
        qrep = jnp.concatenate([jnp.broadcast_to(q4[t:t + 1, :], (N_HEADS, D_MODEL)) for t in range(n_new)], axis=0)
        r = lax.broadcasted_iota(jnp.int32, (rows, D_MODEL), 0)
        c = lax.broadcasted_iota(jnp.int32, (rows, D_MODEL), 1)
        qm = jnp.where((c // HEAD_DIM) == (r % N_HEADS), qrep, 0.0).astype(BF16)
        qm_ref[...] = qm

        er = lax.broadcasted_iota(jnp.int32, (rows, N_HEADS), 0)
        ec = lax.broadcasted_iota(jnp.int32, (rows, N_HEADS), 1)
        expand = ((er % N_HEADS) == ec).astype(F32)
        lfn = lfn_ref[0]
        run = lfn[0:1, :]
        cums = [run]
        for t in range(1, n_new):
            run = run + lfn[t:t + 1, :]
            cums.append(run)
        fn = jnp.concatenate(cums + [jnp.zeros((kpad - n_new, N_HEADS), F32)], axis=0)
        fnt = _dot_nt(expand, fn, precision=HIGHEST)
        trow = lax.broadcasted_iota(jnp.int32, (rows, kpad), 0) // N_HEADS
        tcol = lax.broadcasted_iota(jnp.int32, (rows, kpad), 1)
        fcol = jnp.sum(jnp.where(tcol == trow, fnt, 0.0), axis=-1, keepdims=True)
        fn_ref[...] = fcol
        s = _dot_nt(qm, kn_ref[0]) + (fcol - fnt)
        s = jnp.where(tcol <= trow, s, NEG_INF)
        m = jnp.max(s, axis=-1, keepdims=True)
        p = jnp.exp(s - m)
        m_ref[...] = m
        l_ref[...] = jnp.sum(p, axis=-1, keepdims=True)
        acc_ref[...] = _dot(p.astype(BF16), vn_ref[0])
        carry_ref[...] = jnp.zeros(carry_ref.shape, F32)

    qm = qm_ref[...]
    fcol = fn_ref[...]
    kt = jnp.concatenate([_page_t(k_refs[r], page) for r in range(ppg)], axis=1)
    vt = jnp.concatenate([_page_t(v_refs[r], page) for r in range(ppg)], axis=1)
    decays, carry = _page_decays(lf_refs, carry_ref[...], page)
    carry_ref[...] = carry
    decay = jnp.concatenate([jnp.concatenate([d] * n_new, axis=0) for d in decays], axis=1)
    s = _dot(qm, kt) + (fcol + decay)
    m = m_ref[...]
    m_new = jnp.maximum(m, jnp.max(s, axis=-1, keepdims=True))
    alpha = jnp.exp(m - m_new)
    p = jnp.exp(s - m_new)
    m_ref[...] = m_new
    l_ref[...] = alpha * l_ref[...] + jnp.sum(p, axis=-1, keepdims=True)
    acc_ref[...] = alpha * acc_ref[...] + _dot_nt(p.astype(BF16), vt)

    @pl.when(g == pl.num_programs(1) - 1)
    def _():
        o = acc_ref[...] / l_ref[...]
        r = lax.broadcasted_iota(jnp.int32, (rows, D_MODEL), 0)
        c = lax.broadcasted_iota(jnp.int32, (rows, D_MODEL), 1)
        o = jnp.where((c // HEAD_DIM) == (r % N_HEADS), o, 0.0)
        o_ref[0] = jnp.concatenate(
            [jnp.sum(o[t * N_HEADS:(t + 1) * N_HEADS, :], axis=0, keepdims=True) for t in range(n_new)], axis=0)


def _attn_sample(layer, page_table, q, kn, vn, lfn, cache_kt, cache_vt, cache_lft, *, ppg):
    b, n_new, _ = q.shape
    n_pages = page_table.shape[1]
    page = cache_kt.shape[-1]
    kpad = kn.shape[1]
    rows = n_new * N_HEADS
    groups = n_pages // ppg

    def page_spec(r, lead):
        def imap(s, g, pt):
            return (layer, pt[s * n_pages + (n_pages - 1 - (g * ppg + r))]) + (0,) * (len(lead) + 1)
        return pl.BlockSpec((1, 1) + lead + (page,), imap)

    per_seq = lambda nrow, width: pl.BlockSpec((1, nrow, width), lambda s, g, pt: (s, 0, 0))
    in_specs = ([per_seq(n_new, D_MODEL), per_seq(kpad, D_MODEL), per_seq(kpad, D_MODEL), per_seq(n_new, N_HEADS)]
                + [page_spec(r, (N_HEADS, HEAD_DIM)) for r in range(ppg)]
                + [page_spec(r, (N_HEADS, HEAD_DIM)) for r in range(ppg)]
                + [page_spec(r, (N_HEADS,)) for r in range(ppg)])
    grid_spec = pltpu.PrefetchScalarGridSpec(
        num_scalar_prefetch=1,
        grid=(b, groups),
        in_specs=in_specs,
        out_specs=per_seq(n_new, D_MODEL),
        scratch_shapes=[pltpu.VMEM((rows, D_MODEL), BF16), pltpu.VMEM((rows, 1), F32), pltpu.VMEM((rows, 1), F32),
                        pltpu.VMEM((rows, D_MODEL), F32), pltpu.VMEM((rows, 1), F32), pltpu.VMEM((N_HEADS, 1), F32)],
    )
    return pl.pallas_call(
        functools.partial(_attn_sample_kernel, n_new=n_new, ppg=ppg, page=page),
        grid_spec=grid_spec,
        out_shape=jax.ShapeDtypeStruct((b, n_new, D_MODEL), F32),
        compiler_params=_cparams(("arbitrary", "arbitrary")),
        name="attn_sample",
    )(page_table.reshape(-1), q, kn, vn, lfn, *([cache_kt] * ppg), *([cache_vt] * ppg), *([cache_lft] * ppg))


def kernel(x_prompt, x_sample, c_prompt, c_sample, cache_k, cache_v, cache_logf, state_conv, page_table,
           norm_pre, norm_post, w_mod, b_mod, ffn1_gate, ffn1_up, ffn1_down, w_in, b_forget,
           conv_w, conv_b, conv_ln_g, conv_ln_b, w_pw, w_out, ffn2_gate, ffn2_up, ffn2_down):
    depth = w_mod.shape[0]
    bp, seq, _ = x_prompt.shape
    bs, n_new, _ = x_sample.shape
    n_phys, page = cache_k.shape[1], cache_k.shape[2]
    tp, ts = bp * seq, bs * n_new
    d_attn = N_HEADS * HEAD_DIM
    assert d_attn == D_MODEL and w_in.shape[2] == 3 * d_attn + N_HEADS + 4 * D_MODEL

    tm_ffn, tm_proj, tq, tm_conv, tm_merge = 512, 256, 256, 256, 512
    assert tm_proj == tq
    kpad = 16

    mod = _modulation(jnp.concatenate([c_prompt, c_sample], axis=0), w_mod, b_mod)
    mod = mod.reshape(depth, bp + bs, N_SUB, 3, D_MODEL)

    xp = x_prompt.reshape(tp, D_MODEL)
    xs = x_sample.reshape(ts, D_MODEL)
    outs = [[] for _ in range(8)]
    kv_p, kv_s = (), ()
    cache_kt = cache_k.transpose(0, 1, 3, 4, 2)
    cache_vt = cache_v.transpose(0, 1, 3, 4, 2)
    cache_lft = cache_logf.transpose(0, 1, 3, 2)
    for l in range(depth):
        wqkv = w_in[l, :, :3 * d_attn].astype(BF16)
        wf = w_in[l, :, 3 * d_attn:3 * d_attn + N_HEADS].astype(BF16)
        wr = w_in[l, :, 3 * d_attn + N_HEADS:].astype(BF16)
        bf = b_forget[l].reshape(1, N_HEADS)
        f1 = (ffn1_gate[l].astype(BF16), ffn1_up[l].astype(BF16), ffn1_down[l].astype(BF16))
        f2 = (ffn2_gate[l].astype(BF16), ffn2_up[l].astype(BF16), ffn2_down[l].astype(BF16))
        wpw = w_pw[l].astype(BF16)
        wout = w_out[l].astype(BF16)
        gpre = [norm_pre[l, s].reshape(1, D_MODEL) for s in range(N_SUB)]
        gpost = [norm_post[l, s].reshape(1, D_MODEL) for s in range(N_SUB)]
        cb = conv_b[l].reshape(1, D_MODEL)
        lng = conv_ln_g[l].reshape(1, D_MODEL)
        lnb = conv_ln_b[l].reshape(1, D_MODEL)
        cw3 = conv_w[l].reshape(CONV_W, N_LANE_TILES, LANES).transpose(1, 0, 2)

        mods_p = [[mod[l, :bp, s, j].reshape(bp, 1, D_MODEL) for j in range(3)] for s in range(N_SUB)]
        mods_s = [[jnp.repeat(mod[l, bp:, s, j], n_new, axis=0).reshape(1, ts, D_MODEL) for j in range(3)]
                  for s in range(N_SUB)]

        xp = _ffn(xp, mods_p[0], gpre[0], gpost[0], *f1, tm=tm_ffn, tiles_per_block=seq // tm_ffn, res_w=0.5)
        q, k, v, kb, vt, lf, fc, u, sga, sgc = _proj(
            xp, mods_p[1], gpre[1], wqkv, wf, wr, bf, tm=tm_proj, tiles_per_block=seq // tm_proj,
            tiles_per_seq=seq // tm_proj, q_scale=QK_SCALE * LOG2E, layer=l, depth=depth, kv_stack=kv_p,
            transposed_kv=True)
        kv_p = (k, v)
        fcol = fc.reshape(bp, seq, N_HEADS)
        attn = _attn_prompt(q.reshape(bp, seq, D_MODEL), kb.reshape(bp, seq, D_MODEL),
                            vt.reshape(bp, seq // tq, D_MODEL, tq), fcol.transpose(0, 2, 1), fcol, tq=tq)
        u3 = u.reshape(bp, seq, D_MODEL)
        zs = _conv_prompt(u3, cw3, cb, lng, lnb, tm=tm_conv)
        xp = _merge(xp, mods_p[1][2], gpost[1], zs.reshape(tp, D_MODEL), attn.reshape(tp, D_MODEL), sga, sgc,
                    wpw, wout, tm=tm_merge, tiles_per_block=seq // tm_merge)
        xp = _ffn(xp, mods_p[2], gpre[2], gpost[2], *f2, tm=tm_ffn, tiles_per_block=seq // tm_ffn, res_w=0.5)
        outs[2].append(lf.reshape(bp, seq, N_HEADS))
        outs[3].append(u3[:, seq - (CONV_W - 1):, :])

        xs = _ffn(xs, mods_s[0], gpre[0], gpost[0], *f1, tm=ts, tiles_per_block=1, res_w=0.5)
        q, k, v, kb, vb, lf, _, u, sga, sgc = _proj(
            xs, mods_s[1], gpre[1], wqkv, wf, wr, bf, tm=ts, tiles_per_block=1, tiles_per_seq=1, q_scale=QK_SCALE,
            layer=l, depth=depth, kv_stack=kv_s, transposed_kv=False)
        kv_s = (k, v)
        pad = lambda a: jnp.pad(a.reshape(bs, n_new, D_MODEL), ((0, 0), (0, kpad - n_new), (0, 0)))
        attn = _attn_sample(l, page_table, (q.astype(F32)).reshape(bs, n_new, D_MODEL), pad(kb), pad(vb),
                            lf.reshape(bs, n_new, N_HEADS), cache_kt, cache_vt, cache_lft, ppg=16)
        uext = jnp.concatenate([state_conv[l], u.reshape(bs, n_new, D_MODEL)], axis=1)
        zs = _conv_sample(uext, conv_w[l], cb, lng, lnb, n_new=n_new)
        xs = _merge(xs, mods_s[1][2], gpost[1], zs.reshape(ts, D_MODEL), attn.reshape(ts, D_MODEL), sga, sgc,
                    wpw, wout, tm=ts, tiles_per_block=1)
        xs = _ffn(xs, mods_s[2], gpre[2], gpost[2], *f2, tm=ts, tiles_per_block=1, res_w=0.5)
        outs[6].append(lf.reshape(bs, n_new, N_HEADS))
        outs[7].append(uext[:, n_new:, :])

    lf_p, conv_p, lf_s, conv_s = (jnp.stack(outs[i], axis=0) for i in (2, 3, 6, 7))
    k_p, v_p = (a.reshape(depth, bp, N_HEADS, HEAD_DIM, seq).transpose(0, 1, 4, 2, 3) for a in kv_p)
    k_s, v_s = (a.reshape(depth, bs, n_new, N_HEADS, HEAD_DIM) for a in kv_s)
    return (xp.reshape(bp, seq, D_MODEL), xs.reshape(bs, n_new, D_MODEL), k_p, v_p, lf_p, conv_p, k_s, v_s, lf_s, conv_s)
```
